```python
import jax, jax.numpy as jnp
from jax import lax
import numpy as np

D_MODEL = 1024
BATCH = 8
SEQ = 8192
DEPTH = 1

N_META = 16
GRID_W = 64
DN_HEADS = 4
DN_HEAD_DIM = 128
DN_WIDTH = DN_HEADS * DN_HEAD_DIM
DN_CHUNK = 64
CONV_W = 5
AT_Q_HEADS = 8
AT_KV_HEADS = 2
AT_HEAD_DIM = 64
AT_WIDTH = AT_Q_HEADS * AT_HEAD_DIM
AT_KV_WIDTH = AT_KV_HEADS * AT_HEAD_DIM
Q_BLOCK = 128
ROPE_THETA = 10000.0
ROPE_AXIS_DIM = AT_HEAD_DIM // 2
D_FF = 4 * D_MODEL
EPS = 1e-6
MIX_WIDTH = DN_WIDTH + AT_WIDTH
IN_SIZES = (DN_WIDTH, DN_WIDTH, DN_WIDTH, DN_WIDTH, 2 * DN_HEADS, 2 * DN_HEADS, AT_WIDTH, AT_KV_WIDTH, AT_KV_WIDTH)
IN_COLS = 4 * DN_WIDTH + 4 * DN_HEADS + AT_WIDTH + 2 * AT_KV_WIDTH

kernel_name = "hymba_deltanet_axial_gqa_encoder_block"


def rms_norm(x, w):
    xf = x.astype(jnp.float32)
    y = xf * lax.rsqrt(jnp.mean(xf * xf, axis=-1, keepdims=True) + EPS)
    return (y * w.astype(jnp.float32)).astype(x.dtype)


def l2_norm(x):
    xf = x.astype(jnp.float32)
    return xf * lax.rsqrt(jnp.sum(xf * xf, axis=-1, keepdims=True) + EPS)


def short_conv_silu(x, w):
    C = x.shape[-1]
    y = lax.conv_general_dilated(
        x, w[:, None, :].astype(x.dtype), window_strides=(1,),
        padding=[(CONV_W // 2, CONV_W // 2)],
        dimension_numbers=('NWC', 'WIO', 'NWC'), feature_group_count=C)
    return jax.nn.silu(y)


def to_scan_order(a_fwd, a_bwd):
    pad = jnp.zeros((a_fwd.shape[0], DN_CHUNK - N_META) + a_fwd.shape[2:], a_fwd.dtype)
    fwd = jnp.concatenate([pad, a_fwd], axis=1)
    bwd = jnp.concatenate([pad, a_bwd[:, :N_META], jnp.flip(a_bwd[:, N_META:], axis=1)], axis=1)
    return jnp.concatenate([fwd, bwd], axis=0)


def chunk_gated_delta_rule(q, k, v, g, beta):
    N, T, H, K = q.shape
    V = v.shape[-1]
    C = DN_CHUNK
    n = T // C

    def chunks(a):
        return jnp.moveaxis(a.reshape((N, n, C, H) + a.shape[3:]), 3, 2)

    q, k, v, g, beta = chunks(q), chunks(k), chunks(v), chunks(g), chunks(beta)
    gc = jnp.cumsum(g, axis=-1)
    incl = jnp.tril(jnp.ones((C, C), dtype=bool))
    strict = jnp.tril(jnp.ones((C, C), dtype=bool), -1)
    decay = jnp.exp(jnp.where(incl, gc[..., :, None] - gc[..., None, :], -jnp.inf))
    kk = jnp.einsum('bnhik,bnhjk->bnhij', k, k)
    a_low = jnp.where(strict, kk * decay * beta[..., :, None], 0.0)
    rhs = jnp.concatenate([v * beta[..., None], k * (beta * jnp.exp(gc))[..., None]], axis=-1)
    uw = lax.linalg.triangular_solve(a_low + jnp.eye(C, dtype=a_low.dtype), rhs,
                                     left_side=True, lower=True, unit_diagonal=True)
    u, w = uw[..., :V], uw[..., V:]
    qk = jnp.einsum('bnhik,bnhjk->bnhij', q, k) * decay
    q_dec = q * jnp.exp(gc)[..., None]
    k_dec = k * jnp.exp(gc[..., -1:] - gc)[..., None]
    g_last = jnp.exp(gc[..., -1])

    def step(S, xs):
        q_i, k_i, u_i, w_i, qk_i, gl_i = xs
        v_new = u_i - jnp.einsum('bhck,bhkv->bhcv', w_i, S)
        o_i = jnp.einsum('bhck,bhkv->bhcv', q_i, S) + jnp.einsum('bhij,bhjv->bhiv', qk_i, v_new)
        S = S * gl_i[..., None, None] + jnp.einsum('bhck,bhcv->bhkv', k_i, v_new)
        return S, o_i

    xs = tuple(jnp.moveaxis(a, 1, 0) for a in (q_dec, k_dec, u, w, qk, g_last))
    S0 = jnp.zeros((N, H, K, V), jnp.float32)
    _, o = lax.scan(step, S0, xs)
    o = jnp.moveaxis(o, 0, 1)
    return jnp.moveaxis(o, 3, 2).reshape(N, T, H, V)


def gated_deltanet_group(q, k, v, z, b, a, conv_w, a_log, dt_bias, o_norm_w):
    B, L, _ = q.shape
    out_dtype = q.dtype
    qkv = short_conv_silu(jnp.concatenate([q, k, v], axis=-1), conv_w)
    q, k, v = jnp.split(qkv, 3, axis=-1)
    heads = lambda t: t.reshape(B, L, DN_HEADS, DN_HEAD_DIM)
    q = l2_norm(heads(q)) * (DN_HEAD_DIM ** -0.5)
    k = l2_norm(heads(k))
    v = heads(v).astype(jnp.float32)
    beta = jax.nn.sigmoid(b.astype(jnp.float32)).reshape(B, L, 2, DN_HEADS)
    g = -jnp.exp(a_log.astype(jnp.float32)) * jax.nn.softplus(
        a.astype(jnp.float32).reshape(B, L, 2, DN_HEADS) + dt_bias.astype(jnp.float32))
    o = chunk_gated_delta_rule(
        to_scan_order(q, q), to_scan_order(k, k), to_scan_order(v, v),
        to_scan_order(g[:, :, 0], g[:, :, 1]), to_scan_order(beta[:, :, 0], beta[:, :, 1]))
    pad = DN_CHUNK - N_META
    o_f = o[:B, pad:]
    o_b = o[B:, pad:]
    o_b = jnp.concatenate([o_b[:, :N_META], jnp.flip(o_b[:, N_META:], axis=1)], axis=1)
    o = o_f + o_b
    o = rms_norm(o, o_norm_w) * jax.nn.silu(heads(z).astype(jnp.float32))
    return o.reshape(B, L, DN_WIDTH).astype(out_dtype)


def axial_rope_angles(n_real):
    rows = n_real // GRID_W
    r, c = jnp.meshgrid(jnp.arange(rows), jnp.arange(GRID_W), indexing='ij')
    r = r.reshape(-1).astype(jnp.float32)
    c = c.reshape(-1).astype(jnp.float32)
    F = ROPE_AXIS_DIM // 2
    freqs = ROPE_THETA ** (-jnp.arange(F, dtype=jnp.float32) / F)
    ang = jnp.concatenate([r[:, None] * freqs, c[:, None] * freqs], axis=-1)
    return jnp.concatenate([jnp.zeros((N_META, 2 * F), jnp.float32), ang], axis=0)


def apply_axial_rope(x, ang):
    B, L, H, D = x.shape
    F = ROPE_AXIS_DIM // 2
    xr = x.astype(jnp.float32).reshape(B, L, H, 2, 2, F)
    x1, x2 = xr[..., 0, :], xr[..., 1, :]
    an = ang.reshape(L, 1, 2, F)
    cos, sin = jnp.cos(an), jnp.sin(an)
    out = jnp.stack([x1 * cos - x2 * sin, x2 * cos + x1 * sin], axis=-2)
    return out.reshape(B, L, H, D).astype(x.dtype)


def axial_gqa_group(q, k, v, q_norm_w, k_norm_w):
    B, L, _ = q.shape
    G = AT_Q_HEADS // AT_KV_HEADS
    n_real = L - N_META
    q = rms_norm(q.reshape(B, L, AT_Q_HEADS, AT_HEAD_DIM), q_norm_w)
    k = rms_norm(k.reshape(B, L, AT_KV_HEADS, AT_HEAD_DIM), k_norm_w)
    v = v.reshape(B, L, AT_KV_HEADS, AT_HEAD_DIM)
    ang = axial_rope_angles(n_real)
    q = apply_axial_rope(q, ang) * (AT_HEAD_DIM ** -0.5)
    k = apply_axial_rope(k, ang)
    q = q.reshape(B, L, AT_KV_HEADS, G, AT_HEAD_DIM)

    def attend(qb):
        s = jnp.einsum('bqhgd,bkhd->bhgqk', qb, k, preferred_element_type=jnp.float32)
        p = jax.nn.softmax(s, axis=-1).astype(v.dtype)
        return jnp.einsum('bhgqk,bkhd->bqhgd', p, v)

    o_meta = attend(q[:, :N_META]).reshape(B, N_META, AT_WIDTH)
    nb = n_real // Q_BLOCK
    qr = q[:, N_META:].reshape(B, nb, Q_BLOCK, AT_KV_HEADS, G, AT_HEAD_DIM)
    o_real = lax.map(attend, jnp.moveaxis(qr, 1, 0))
    o_real = jnp.moveaxis(o_real, 0, 1).reshape(B, n_real, AT_WIDTH)
    return jnp.concatenate([o_meta, o_real], axis=1)


def setup_inputs(seed: int = 0) -> dict:
    key = jax.random.key(seed)
    ks = jax.random.split(key, 20)
    f32 = jnp.float32
    nrm = lambda k_, shape, scale: jax.random.normal(k_, shape, f32) * scale
    gain = lambda k_, shape: 1.0 + 0.02 * jax.random.normal(k_, shape, f32)
    x = jax.random.normal(ks[0], (BATCH, SEQ, D_MODEL), f32)
    meta_tokens = nrm(ks[1], (N_META, D_MODEL), 1.0)
    w_in = nrm(ks[2], (DEPTH, D_MODEL, IN_COLS), D_MODEL ** -0.5)
    conv_w = nrm(ks[3], (DEPTH, CONV_W, 3 * DN_WIDTH), CONV_W ** -0.5)
    a_log = jnp.log(jax.random.uniform(ks[4], (DEPTH, 2, DN_HEADS), f32, 1.0, 16.0))
    dt = jnp.exp(jax.random.uniform(ks[5], (DEPTH, 2, DN_HEADS), f32, np.log(1e-3), np.log(1e-1)))
    dt_bias = dt + jnp.log(-jnp.expm1(-dt))
    dn_out_norm = gain(ks[6], (DEPTH, DN_HEAD_DIM))
    q_norm = gain(ks[7], (DEPTH, AT_HEAD_DIM))
    k_norm = gain(ks[8], (DEPTH, AT_HEAD_DIM))
    w_out = nrm(ks[9], (DEPTH, MIX_WIDTH, D_MODEL), MIX_WIDTH ** -0.5)
    norm_mix_pre = gain(ks[10], (DEPTH, D_MODEL))
    norm_mix_post = gain(ks[11], (DEPTH, D_MODEL))
    w_up = nrm(ks[12], (DEPTH, D_MODEL, D_FF), D_MODEL ** -0.5)
    w_down = nrm(ks[13], (DEPTH, D_FF, D_MODEL), D_FF ** -0.5)
    norm_mlp_pre = gain(ks[14], (DEPTH, D_MODEL))
    norm_mlp_post = gain(ks[15], (DEPTH, D_MODEL))
    return {"x": x, "meta_tokens": meta_tokens, "w_in": w_in, "conv_w": conv_w,
            "a_log": a_log, "dt_bias": dt_bias, "dn_out_norm": dn_out_norm,
            "q_norm": q_norm, "k_norm": k_norm, "w_out": w_out,
            "norm_mix_pre": norm_mix_pre, "norm_mix_post": norm_mix_post,
            "w_up": w_up, "w_down": w_down,
            "norm_mlp_pre": norm_mlp_pre, "norm_mlp_post": norm_mlp_post}


def reference(x, meta_tokens, w_in, conv_w, a_log, dt_bias, dn_out_norm, q_norm, k_norm, w_out,
              norm_mix_pre, norm_mix_post, w_up, w_down, norm_mlp_pre, norm_mlp_post):
    B = x.shape[0]
    split_points = [int(s) for s in np.cumsum(IN_SIZES)[:-1]]
    meta = jnp.broadcast_to(meta_tokens[None].astype(x.dtype), (B, N_META, D_MODEL))
    h = jnp.concatenate([meta, x], axis=1)
    for l in range(DEPTH):
        u = rms_norm(h, norm_mix_pre[l])
        proj = u @ w_in[l]
        dq, dk, dv, dz, db, da, aq, ak, av = jnp.split(proj, split_points, axis=-1)
        o_dn = gated_deltanet_group(dq, dk, dv, dz, db, da, conv_w[l], a_log[l], dt_bias[l], dn_out_norm[l])
        o_at = axial_gqa_group(aq, ak, av, q_norm[l], k_norm[l])
        mix = jnp.concatenate([o_dn, o_at.astype(o_dn.dtype)], axis=-1) @ w_out[l]
        h = h + rms_norm(mix, norm_mix_post[l])
        u = rms_norm(h, norm_mlp_pre[l])
        f = jnp.square(jax.nn.relu(u @ w_up[l])) @ w_down[l]
        h = h + rms_norm(f, norm_mlp_post[l])
    return h[:, N_META:]
```

```python
import functools
import math

import numpy as np
import jax
import jax.numpy as jnp
from jax import lax
from jax.experimental import pallas as pl
from jax.experimental.pallas import tpu as pltpu

F32 = jnp.float32
BF16 = jnp.bfloat16

D_MODEL = 1024
N_META = 16
GRID_W = 64
DN_HEADS = 4
DN_HEAD_DIM = 128
DN_WIDTH = DN_HEADS * DN_HEAD_DIM
CONV_W = 5
AT_Q_HEADS = 8
AT_KV_HEADS = 2
AT_HEAD_DIM = 64
AT_WIDTH = AT_Q_HEADS * AT_HEAD_DIM
AT_KV_WIDTH = AT_KV_HEADS * AT_HEAD_DIM
ROPE_THETA = 10000.0
ROPE_AXIS_DIM = AT_HEAD_DIM // 2
D_FF = 4 * D_MODEL
EPS = 1e-6

LANES = 128
ROW_TILE = 512
PAD_ROWS = ROW_TILE
DN_CHUNK = 128
META_CHUNK = PAD_ROWS // DN_CHUNK - 1
GATE_W = LANES
VMEM_LIMIT = 56 * 1024 * 1024


def _cparams(sem):
    return pltpu.CompilerParams(dimension_semantics=sem, vmem_limit_bytes=VMEM_LIMIT)


def _split_bf16(x):
    hi = x.astype(BF16)
    lo = (x - hi.astype(F32)).astype(BF16)
    return hi, lo


def _group_mean(sq, ones_bd):
    hi, lo = _split_bf16(sq)
    return (jnp.dot(hi, ones_bd, preferred_element_type=F32)
            + jnp.dot(lo, ones_bd, preferred_element_type=F32))


def _block_diag_ones(width, group):
    idx = np.arange(width) // group
    return jnp.asarray((idx[:, None] == idx[None, :]).astype(np.float32) / group, dtype=BF16)


def _sigmoid(x):
    return 1.0 / (1.0 + jnp.exp(-x))


def _rope(x, cos, sin_signed, width):
    half = ROPE_AXIS_DIM // 2
    lane = lax.broadcasted_iota(jnp.int32, x.shape, 1)
    partner = jnp.where((lane % ROPE_AXIS_DIM) < half,
                        pltpu.roll(x, width - half, 1),
                        pltpu.roll(x, half, 1))
    return x * cos + partner * sin_signed


def _inproj_kernel(x_ref, metap_ref, wpre_ref, wqkv_ref, wz_ref, wba_ref, wq_ref, wk_ref, wv_ref,
                   qn_ref, kn_ref, cos_ref, sin_ref, onesq_ref, onesk_ref,
                   qkv_ref, z_ref, ba_ref, aq_ref, ak_ref, av_ref):
    i = pl.program_id(1)
    h = jnp.where(i == 0, metap_ref[...], x_ref[0])
    ms = jnp.mean(h * h, axis=-1, keepdims=True)
    u = (h * lax.rsqrt(ms + EPS) * wpre_ref[...]).astype(BF16)

    qkv_ref[0] = jnp.dot(u, wqkv_ref[...], preferred_element_type=F32)
    z_ref[0] = jnp.dot(u, wz_ref[...], preferred_element_type=F32)
    ba_ref[0] = jnp.dot(u, wba_ref[...], preferred_element_type=F32)

    cos2 = cos_ref[...]
    sin2 = sin_ref[...]
    q = jnp.dot(u, wq_ref[...], preferred_element_type=F32)
    q = q * lax.rsqrt(_group_mean(q * q, onesq_ref[...]) + EPS) * qn_ref[...]
    reps = AT_WIDTH // LANES
    q = _rope(q, jnp.concatenate([cos2] * reps, axis=1), jnp.concatenate([sin2] * reps, axis=1), AT_WIDTH)
    aq_ref[0] = (q * (AT_HEAD_DIM ** -0.5)).astype(BF16)

    k = jnp.dot(u, wk_ref[...], preferred_element_type=F32)
    k = k * lax.rsqrt(_group_mean(k * k, onesk_ref[...]) + EPS) * kn_ref[...]
    k = _rope(k, cos2, sin2, AT_KV_WIDTH).astype(BF16)
    ak_ref[0] = jnp.concatenate([k[:, :AT_HEAD_DIM], k[:, :AT_HEAD_DIM],
                                 k[:, AT_HEAD_DIM:], k[:, AT_HEAD_DIM:]], axis=1)

    v = jnp.dot(u, wv_ref[...], preferred_element_type=F32).astype(BF16)
    lane = lax.broadcasted_iota(jnp.int32, (v.shape[0], AT_HEAD_DIM), 1)
    tail = jnp.where(lane == 0, 1.0, 0.0).astype(BF16)
    av_ref[0] = jnp.concatenate([v[:, :AT_HEAD_DIM], tail, v[:, AT_HEAD_DIM:], tail], axis=1)


def _rope_tables(seq):
    f = ROPE_AXIS_DIM // 2
    t = jnp.arange(seq)
    r = (t // GRID_W).astype(F32)
    c = (t % GRID_W).astype(F32)
    freqs = ROPE_THETA ** (-jnp.arange(f, dtype=F32) / f)
    ang = jnp.concatenate([r[:, None] * freqs, c[:, None] * freqs], axis=-1)
    ang = jnp.concatenate([jnp.zeros((PAD_ROWS, 2 * f), F32), ang], axis=0)
    cos, sin = jnp.cos(ang), jnp.sin(ang)
    cos_h = jnp.concatenate([cos[:, :f], cos[:, :f], cos[:, f:], cos[:, f:]], axis=1)
    sin_h = jnp.concatenate([-sin[:, :f], sin[:, :f], -sin[:, f:], sin[:, f:]], axis=1)
    return jnp.concatenate([cos_h, cos_h], axis=1), jnp.concatenate([sin_h, sin_h], axis=1)


def _in_projection(x, meta_tokens, w_in, norm_pre, q_norm, k_norm):
    bsz, seq, _ = x.shape
    lp = PAD_ROWS + seq
    nt = lp // ROW_TILE
    o = np.cumsum([0, DN_WIDTH * 3, DN_WIDTH, 2 * DN_HEADS, 2 * DN_HEADS, AT_WIDTH, AT_KV_WIDTH, AT_KV_WIDTH])
    wb = w_in.astype(BF16)
    w_qkv, w_z = wb[:, o[0]:o[1]], wb[:, o[1]:o[2]]
    w_ba = jnp.pad(wb[:, o[2]:o[4]], ((0, 0), (0, GATE_W - 4 * DN_HEADS)))
    w_q, w_k, w_v = wb[:, o[4]:o[5]], wb[:, o[5]:o[6]], wb[:, o[6]:o[7]]
    metap = jnp.pad(meta_tokens.astype(F32), ((PAD_ROWS - N_META, 0), (0, 0)))
    cos2, sin2 = _rope_tables(seq)
    qn = jnp.tile(q_norm.astype(F32), AT_Q_HEADS)[None]
    kn = jnp.tile(k_norm.astype(F32), AT_KV_HEADS)[None]

    def full(a):
        return pl.BlockSpec(a.shape, lambda b, i: (0,) * a.ndim)

    def rows(width):
        return pl.BlockSpec((1, ROW_TILE, width), lambda b, i: (b, i, 0))

    wpre = norm_pre.astype(F32)[None]
    ones_q = _block_diag_ones(AT_WIDTH, AT_HEAD_DIM)
    ones_k = _block_diag_ones(AT_KV_WIDTH, AT_HEAD_DIM)
    consts = (metap, wpre, w_qkv, w_z, w_ba, w_q, w_k, w_v, qn, kn)
    widths = (3 * DN_WIDTH, DN_WIDTH, GATE_W, AT_WIDTH, 2 * LANES, 2 * LANES)
    dtypes = (F32, F32, F32, BF16, BF16, BF16)
    return pl.pallas_call(
        _inproj_kernel,
        grid=(bsz, nt),
        in_specs=[pl.BlockSpec((1, ROW_TILE, D_MODEL), lambda b, i: (b, jnp.maximum(i - 1, 0), 0))]
                 + [full(a) for a in consts]
                 + [pl.BlockSpec((ROW_TILE, LANES), lambda b, i: (i, 0))] * 2
                 + [full(ones_q), full(ones_k)],
        out_specs=[rows(w) for w in widths],
        out_shape=[jax.ShapeDtypeStruct((bsz, lp, w), d) for w, d in zip(widths, dtypes)],
        compiler_params=_cparams(("parallel", "parallel")),
        name="in_projection",
    )(x, *consts, cos2, sin2, ones_q, ones_k)


HALO = 8


def _softplus(x):
    return jnp.maximum(x, 0.0) + jnp.log1p(jnp.exp(-jnp.abs(x)))


def _conv_kernel(x_ref, prev_ref, next_ref, ba_ref, cw_ref, alog_ref, dtb_ref, ones_ref,
                 q_ref, k_ref, v_ref, g_ref, xe_ref):
    i = pl.program_id(1)
    last = pl.num_programs(1) - 1
    t = ROW_TILE
    xe_ref[0:HALO] = jnp.where(i == 0, 0.0, prev_ref[0])
    xe_ref[HALO:HALO + t] = x_ref[0]
    xe_ref[HALO + t:2 * HALO + t] = jnp.where(i == last, 0.0, next_ref[0])
    acc = None
    for j in range(CONV_W):
        term = xe_ref[pl.ds(HALO - CONV_W // 2 + j, t), :] * cw_ref[j:j + 1, :]
        acc = term if acc is None else acc + term
    y = acc * _sigmoid(acc)

    row = lax.broadcasted_iota(jnp.int32, (t, 1), 0)
    valid = jnp.logical_or(i > 0, row >= PAD_ROWS - N_META)

    ones = ones_ref[...]
    q = y[:, :DN_WIDTH]
    k = y[:, DN_WIDTH:2 * DN_WIDTH]
    v = y[:, 2 * DN_WIDTH:]
    q = q * lax.rsqrt(_group_mean(q * q, ones) * DN_HEAD_DIM + EPS) * (DN_HEAD_DIM ** -0.5)
    k = k * lax.rsqrt(_group_mean(k * k, ones) * DN_HEAD_DIM + EPS)
    q_ref[0] = jnp.where(valid, q, 0.0)
    k_ref[0] = jnp.where(valid, k, 0.0)
    v_ref[0] = jnp.where(valid, v, 0.0)

    raw = ba_ref[0]
    lane = lax.broadcasted_iota(jnp.int32, raw.shape, 1)
    beta = _sigmoid(raw)
    g = -jnp.exp(alog_ref[...]) * _softplus(raw + dtb_ref[...])
    gates = jnp.where(lane < 2 * DN_HEADS, beta, jnp.where(lane < 4 * DN_HEADS, g, 0.0))
    g_ref[0] = jnp.where(valid, gates, 0.0)


def _conv_gates(qkv, ba, conv_w, a_log, dt_bias):
    bsz, lp, width = qkv.shape
    nt = lp // ROW_TILE
    per = ROW_TILE // HALO
    cw = jnp.pad(conv_w.astype(F32), ((0, 8 - CONV_W), (0, 0)))
    pad = (2 * DN_HEADS, GATE_W - 4 * DN_HEADS)
    alog = jnp.pad(a_log.astype(F32).reshape(-1), pad)[None]
    dtb = jnp.pad(dt_bias.astype(F32).reshape(-1), pad)[None]
    ones = _block_diag_ones(DN_WIDTH, DN_HEAD_DIM)

    def full(a):
        return pl.BlockSpec(a.shape, lambda b, i: (0,) * a.ndim)

    def rows(w):
        return pl.BlockSpec((1, ROW_TILE, w), lambda b, i: (b, i, 0))

    return pl.pallas_call(
        _conv_kernel,
        grid=(bsz, nt),
        in_specs=[rows(width),
                  pl.BlockSpec((1, HALO, width), lambda b, i: (b, jnp.maximum(i * per - 1, 0), 0)),
                  pl.BlockSpec((1, HALO, width), lambda b, i: (b, jnp.minimum((i + 1) * per, nt * per - 1), 0)),
                  rows(GATE_W), full(cw), full(alog), full(dtb), full(ones)],
        out_specs=[rows(DN_WIDTH)] * 3 + [rows(GATE_W)],
        out_shape=[jax.ShapeDtypeStruct((bsz, lp, DN_WIDTH), F32)] * 3
                  + [jax.ShapeDtypeStruct((bsz, lp, GATE_W), F32)],
        scratch_shapes=[pltpu.VMEM((ROW_TILE + 2 * HALO, width), F32)],
        compiler_params=_cparams(("parallel", "parallel")),
        name="conv_gates",
    )(qkv, qkv, qkv, ba, cw, alog, dtb, ones)


def _dot(a, b):
    return jnp.dot(a.astype(BF16), b.astype(BF16), preferred_element_type=F32)


def _dot_nt(a, b):
    return lax.dot_general(a.astype(BF16), b.astype(BF16), (((1,), (1,)), ((), ())),
                           preferred_element_type=F32)


def _unit_tri_inverse(neg_a, row, col):
    c = DN_CHUNK
    blk = 32
    same = lambda w: (row // w) == (col // w)
    b0 = jnp.where(same(blk), neg_a, 0.0)
    eye = jnp.where(row == col, 1.0, 0.0)
    p = eye + b0
    x = b0
    steps = int(math.log2(blk)) - 1
    for _ in range(steps):
        x = _dot(x, x)
        p = p + _dot(p, x)
    w = blk
    while w < c:
        off = jnp.where(jnp.logical_and(same(2 * w), jnp.logical_not(same(w))), neg_a, 0.0)
        p = p + _dot(_dot(p, off), p)
        w *= 2
    return p - eye


def _delta_kernel(qf_ref, kf_ref, vf_ref, gf_ref, qb_ref, kb_ref, vb_ref, gb_ref,
                  of_ref, ob_ref, s_ref):
    s = pl.program_id(1)

    @pl.when(s == 0)
    def _():
        s_ref[...] = jnp.zeros_like(s_ref)

    c = DN_CHUNK
    row = lax.broadcasted_iota(jnp.int32, (c, c), 0)
    col = lax.broadcasted_iota(jnp.int32, (c, c), 1)
    for d, (q_ref, k_ref, v_ref, g_ref, o_ref) in enumerate(
            ((qf_ref, kf_ref, vf_ref, gf_ref, of_ref), (qb_ref, kb_ref, vb_ref, gb_ref, ob_ref))):
        if d == 0:
            order = col - row
        else:
            order = (col - row) * jnp.where(s == 0, 1, -1)
        incl = order <= 0
        strict = order < 0
        lmat = jnp.where(incl, 1.0, 0.0).astype(BF16)
        gates = g_ref[0]
        g_hi, g_lo = _split_bf16(gates)
        gc_all = (jnp.dot(lmat, g_hi, preferred_element_type=F32)
                  + jnp.dot(lmat, g_lo, preferred_element_type=F32))
        for h in range(DN_HEADS):
            sl = slice(h * DN_HEAD_DIM, (h + 1) * DN_HEAD_DIM)
            jb = d * DN_HEADS + h
            jg = 2 * DN_HEADS + jb
            beta = gates[:, jb:jb + 1]
            gcol = gates[:, jg:jg + 1]
            gc = gc_all[:, jg:jg + 1]
            if d == 0:
                gc_last = gc[c - 1:c, :]
            else:
                gc_last = jnp.where(s == 0, gc[c - 1:c, :], gc[0:1, :])
            gu = jnp.where(strict, gcol, 0.0)
            gu_hi, gu_lo = _split_bf16(gu)
            dmat = (jnp.dot(lmat, gu_hi, preferred_element_type=F32)
                    + jnp.dot(lmat, gu_lo, preferred_element_type=F32))
            decay = jnp.where(incl, jnp.exp(dmat), 0.0)
            egc = jnp.exp(gc)
            q = q_ref[0, :, sl]
            k = k_ref[0, :, sl]
            v = v_ref[0, :, sl]
            kk = _dot_nt(k, k)
            qk = _dot_nt(q, k) * decay
            neg_a = jnp.where(strict, -(kk * decay * beta), 0.0)
            tn = _unit_tri_inverse(neg_a, row, col)
            rhs = jnp.concatenate([v * beta, k * (beta * egc)], axis=1)
            uw = rhs + _dot(tn, rhs)
            u = uw[:, :DN_HEAD_DIM]
            w = uw[:, DN_HEAD_DIM:]
            state = s_ref[jb]
            wq = _dot(jnp.concatenate([w, q * egc], axis=0), state)
            v_new = u - wq[:c]
            o_ref[0, :, sl] = wq[c:] + _dot(qk, v_new)
            k_dec = k * jnp.exp(gc_last - gc)
            s_ref[jb] = state * jnp.exp(gc_last) + _dot(k_dec.T, v_new)


def _delta_rule(q, k, v, gates, seq):
    bsz, lp, _ = q.shape
    ns = 1 + seq // DN_CHUNK

    def fwd(b, s):
        return (b, META_CHUNK + s, 0)

    def bwd(b, s):
        return (b, jnp.where(s == 0, META_CHUNK, META_CHUNK + ns - s), 0)

    def spec(w, imap):
        return pl.BlockSpec((1, DN_CHUNK, w), imap)

    ins = [spec(DN_WIDTH, fwd)] * 3 + [spec(GATE_W, fwd)] + [spec(DN_WIDTH, bwd)] * 3 + [spec(GATE_W, bwd)]
    return pl.pallas_call(
        _delta_kernel,
        grid=(bsz, ns),
        in_specs=ins,
        out_specs=[spec(DN_WIDTH, fwd), spec(DN_WIDTH, bwd)],
        out_shape=[jax.ShapeDtypeStruct((bsz, lp, DN_WIDTH), F32)] * 2,
        scratch_shapes=[pltpu.VMEM((2 * DN_HEADS, DN_HEAD_DIM, DN_HEAD_DIM), F32)],
        compiler_params=_cparams(("parallel", "arbitrary")),
        name="delta_rule",
    )(q, k, v, gates, q, k, v, gates)


KEY_START = PAD_ROWS - LANES
GROUP = AT_Q_HEADS // AT_KV_HEADS


def _kv_tile(n_keys):
    units = n_keys // LANES
    best = max(u for u in range(1, 6) if units % u == 0)
    return best * LANES


def _attn_kernel(q_ref, k_ref, v_ref, o_ref, qm_ref, m_ref, acc_ref, *, tk, nkb):
    tq = q_ref.shape[1]
    lane = lax.broadcasted_iota(jnp.int32, (tq, LANES), 1)
    for j in range(GROUP):
        pair = q_ref[0, :, (j // 2) * LANES:(j // 2 + 1) * LANES]
        keep = (lane >= AT_HEAD_DIM) if j % 2 else (lane < AT_HEAD_DIM)
        qm_ref[j] = jnp.where(keep, pair, jnp.zeros_like(pair))

    def scores(j, k):
        return lax.dot_general(qm_ref[j], k, (((1,), (1,)), ((), ())), preferred_element_type=F32)

    def row_max(s):
        m = s[:, :LANES]
        for t in range(1, tk // LANES):
            m = jnp.maximum(m, s[:, t * LANES:(t + 1) * LANES])
        return jnp.max(m, axis=1, keepdims=True)

    def widen(m):
        return jnp.concatenate([m] * (tk // LANES), axis=1)

    k = k_ref[0, KEY_START:KEY_START + tk, :]
    v = v_ref[0, KEY_START:KEY_START + tk, :]
    is_key = lax.broadcasted_iota(jnp.int32, (tq, tk), 1) >= PAD_ROWS - N_META - KEY_START
    for j in range(GROUP):
        s = jnp.where(is_key, scores(j, k), -jnp.inf)
        m = jnp.broadcast_to(row_max(s), (tq, LANES))
        p = jnp.exp(s - widen(m))
        m_ref[j] = m
        acc_ref[j] = jnp.dot(p.astype(BF16), v, preferred_element_type=F32)

    def body(kb, carry):
        start = pl.multiple_of(KEY_START + kb * tk, LANES)
        k = k_ref[0, pl.ds(start, tk), :]
        v = v_ref[0, pl.ds(start, tk), :]
        for j in range(GROUP):
            s = scores(j, k)
            m_prev = m_ref[j]
            m_new = jnp.maximum(m_prev, row_max(s))
            alpha = jnp.exp(m_prev - m_new)
            p = jnp.exp(s - widen(m_new))
            m_ref[j] = m_new
            acc_ref[j] = alpha * acc_ref[j] + jnp.dot(p.astype(BF16), v, preferred_element_type=F32)
        return carry

    lax.fori_loop(1, nkb, body, 0)

    for j in range(GROUP):
        acc = acc_ref[j]
        inv = 1.0 / acc[:, AT_HEAD_DIM:AT_HEAD_DIM + 1]
        o_ref[0, :, j * AT_HEAD_DIM:(j + 1) * AT_HEAD_DIM] = (acc[:, :AT_HEAD_DIM] * inv).astype(o_ref.dtype)


def _attention(aq, ak, av, seq):
    bsz, lp, _ = aq.shape
    tq = ROW_TILE
    n_keys = lp - KEY_START
    tk = _kv_tile(n_keys)
    gw = GROUP * AT_HEAD_DIM
    return pl.pallas_call(
        functools.partial(_attn_kernel, tk=tk, nkb=n_keys // tk),
        grid=(bsz, AT_KV_HEADS, seq // tq),
        in_specs=[pl.BlockSpec((1, tq, gw), lambda b, g, i: (b, i + PAD_ROWS // tq, g)),
                  pl.BlockSpec((1, lp, LANES), lambda b, g, i: (b, 0, g)),
                  pl.BlockSpec((1, lp, LANES), lambda b, g, i: (b, 0, g))],
        out_specs=pl.BlockSpec((1, tq, gw), lambda b, g, i: (b, i, g)),
        out_shape=jax.ShapeDtypeStruct((bsz, seq, AT_WIDTH), BF16),
        scratch_shapes=[pltpu.VMEM((GROUP, tq, LANES), BF16),
                        pltpu.VMEM((GROUP, tq, LANES), F32),
                        pltpu.VMEM((GROUP, tq, LANES), F32)],
        compiler_params=_cparams(("parallel", "parallel", "parallel")),
        name="gqa_attention",
    )(aq, ak, av)


FF_TILE = 1024


def _rms(x, w):
    return x * lax.rsqrt(jnp.mean(x * x, axis=-1, keepdims=True) + EPS) * w


def _out_mlp_kernel(x_ref, of_ref, ob_ref, z_ref, at_ref, dnw_ref, ones_ref, wod_ref, woa_ref, npost_ref,
                    nmpre_ref, wup_ref, wdn_ref, nmpost_ref, out_ref):
    o = of_ref[0] + ob_ref[0]
    o = o * lax.rsqrt(_group_mean(o * o, ones_ref[...]) + EPS) * dnw_ref[...]
    z = z_ref[0]
    dn = (o * (z * _sigmoid(z))).astype(BF16)
    mix = (jnp.dot(dn, wod_ref[...], preferred_element_type=F32)
           + jnp.dot(at_ref[0], woa_ref[...], preferred_element_type=F32))
    h1 = x_ref[0] + _rms(mix, npost_ref[...])
    u = _rms(h1, nmpre_ref[...]).astype(BF16)
    f = None
    for j in range(D_FF // FF_TILE):
        a = jnp.maximum(jnp.dot(u, wup_ref[:, j * FF_TILE:(j + 1) * FF_TILE], preferred_element_type=F32), 0.0)
        part = jnp.dot((a * a).astype(BF16), wdn_ref[j * FF_TILE:(j + 1) * FF_TILE, :],
                       preferred_element_type=F32)
        f = part if f is None else f + part
    out_ref[0] = h1 + _rms(f, nmpost_ref[...])


def _out_mlp(x, o_f, o_b, z, at, dn_out_norm, w_out, norm_mix_post, norm_mlp_pre, w_up, w_down, norm_mlp_post):
    bsz, seq, _ = x.shape
    off = PAD_ROWS // ROW_TILE
    dnw = jnp.tile(dn_out_norm.astype(F32), DN_HEADS)[None]
    ones = _block_diag_ones(DN_WIDTH, DN_HEAD_DIM)
    wo = w_out.astype(BF16)
    consts = (dnw, ones, wo[:DN_WIDTH], wo[DN_WIDTH:], norm_mix_post.astype(F32)[None],
              norm_mlp_pre.astype(F32)[None], w_up.astype(BF16), w_down.astype(BF16),
              norm_mlp_post.astype(F32)[None])

    def full(a):
        return pl.BlockSpec(a.shape, lambda b, i: (0,) * a.ndim, pipeline_mode=pl.Buffered(1))

    def real(w):
        return pl.BlockSpec((1, ROW_TILE, w), lambda b, i: (b, i, 0))

    def padded(w):
        return pl.BlockSpec((1, ROW_TILE, w), lambda b, i: (b, i + off, 0))

    return pl.pallas_call(
        _out_mlp_kernel,
        grid=(bsz, seq // ROW_TILE),
        in_specs=[real(D_MODEL), padded(DN_WIDTH), padded(DN_WIDTH), padded(DN_WIDTH), real(AT_WIDTH)]
                 + [full(a) for a in consts],
        out_specs=real(D_MODEL),
        out_shape=jax.ShapeDtypeStruct((bsz, seq, D_MODEL), x.dtype),
        compiler_params=_cparams(("parallel", "parallel")),
        name="out_mlp",
    )(x, o_f, o_b, z, at, *consts)


def kernel(x, meta_tokens, w_in, conv_w, a_log, dt_bias, dn_out_norm, q_norm, k_norm, w_out,
           norm_mix_pre, norm_mix_post, w_up, w_down, norm_mlp_pre, norm_mlp_post):
    assert w_in.shape[0] == 1, "single-layer block: meta-token outputs are not materialised"
    bsz, seq, d = x.shape
    assert d == D_MODEL and seq % ROW_TILE == 0 and seq % GRID_W == 0
    qkv, z, ba, aq, ak, av = _in_projection(x, meta_tokens, w_in[0], norm_mix_pre[0], q_norm[0], k_norm[0])
    dq, dk, dv, gates = _conv_gates(qkv, ba, conv_w[0], a_log[0], dt_bias[0])
    o_f, o_b = _delta_rule(dq, dk, dv, gates, seq)
    at = _attention(aq, ak, av, seq)
    return _out_mlp(x, o_f, o_b, z, at, dn_out_norm[0], w_out[0], norm_mix_post[0], norm_mlp_pre[0],
                    w_up[0], w_down[0], norm_mlp_post[0])
```

```python
import functools
import math

import numpy as np
import jax
import jax.numpy as jnp
from jax import lax
from jax.experimental import pallas as pl
from jax.experimental.pallas import tpu as pltpu

F32 = jnp.float32
BF16 = jnp.bfloat16

D_MODEL = 1024
N_META = 16
GRID_W = 64
DN_HEADS = 4
DN_HEAD_DIM = 128
DN_WIDTH = DN_HEADS * DN_HEAD_DIM
CONV_W = 5
AT_Q_HEADS = 8
AT_KV_HEADS = 2
AT_HEAD_DIM = 64
AT_WIDTH = AT_Q_HEADS * AT_HEAD_DIM
AT_KV_WIDTH = AT_KV_HEADS * AT_HEAD_DIM
ROPE_THETA = 10000.0
ROPE_AXIS_DIM = AT_HEAD_DIM // 2
D_FF = 4 * D_MODEL
EPS = 1e-6

LANES = 128
ROW_TILE = 512
PAD_ROWS = ROW_TILE
DN_CHUNK = 128
META_CHUNK = PAD_ROWS // DN_CHUNK - 1
GATE_W = LANES
VMEM_LIMIT = 56 * 1024 * 1024


def _cparams(sem):
    return pltpu.CompilerParams(dimension_semantics=sem, vmem_limit_bytes=VMEM_LIMIT)


def _split_bf16(x):
    hi = x.astype(BF16)
    lo = (x - hi.astype(F32)).astype(BF16)
    return hi, lo


def _group_mean(sq, ones_bd):
    hi, lo = _split_bf16(sq)
    return (jnp.dot(hi, ones_bd, preferred_element_type=F32)
            + jnp.dot(lo, ones_bd, preferred_element_type=F32))


def _block_diag_ones(width, group):
    idx = np.arange(width) // group
    return jnp.asarray((idx[:, None] == idx[None, :]).astype(np.float32) / group, dtype=BF16)


def _sigmoid(x):
    return 1.0 / (1.0 + jnp.exp(-x))


def _rope(x, cos, sin_signed, width):
    half = ROPE_AXIS_DIM // 2
    lane = lax.broadcasted_iota(jnp.int32, x.shape, 1)
    partner = jnp.where((lane % ROPE_AXIS_DIM) < half,
                        pltpu.roll(x, width - half, 1),
                        pltpu.roll(x, half, 1))
    return x * cos + partner * sin_signed


def _inproj_kernel(x_ref, metap_ref, wpre_ref, wqkv_ref, wz_ref, wba_ref, wq_ref, wk_ref, wv_ref,
                   qn_ref, kn_ref, cos_ref, sin_ref, onesq_ref, onesk_ref,
                   qkv_ref, z_ref, ba_ref, aq_ref, ak_ref, av_ref):
    i = pl.program_id(1)
    h = jnp.where(i == 0, metap_ref[...], x_ref[0])
    ms = jnp.mean(h * h, axis=-1, keepdims=True)
    u = (h * lax.rsqrt(ms + EPS) * wpre_ref[...]).astype(BF16)

    qkv_ref[0] = jnp.dot(u, wqkv_ref[...], preferred_element_type=F32)
    z_ref[0] = jnp.dot(u, wz_ref[...], preferred_element_type=F32)
    ba_ref[0] = jnp.dot(u, wba_ref[...], preferred_element_type=F32)

    cos2 = cos_ref[...]
    sin2 = sin_ref[...]
    q = jnp.dot(u, wq_ref[...], preferred_element_type=F32)
    q = q * lax.rsqrt(_group_mean(q * q, onesq_ref[...]) + EPS) * qn_ref[...]
    reps = AT_WIDTH // LANES
    q = _rope(q, jnp.concatenate([cos2] * reps, axis=1), jnp.concatenate([sin2] * reps, axis=1), AT_WIDTH)
    aq_ref[0] = (q * (AT_HEAD_DIM ** -0.5 * math.log2(math.e))).astype(BF16)

    k = jnp.dot(u, wk_ref[...], preferred_element_type=F32)
    k = k * lax.rsqrt(_group_mean(k * k, onesk_ref[...]) + EPS) * kn_ref[...]
    k = _rope(k, cos2, sin2, AT_KV_WIDTH).astype(BF16)
    ak_ref[0] = jnp.concatenate([k[:, :AT_HEAD_DIM], k[:, :AT_HEAD_DIM],
                                 k[:, AT_HEAD_DIM:], k[:, AT_HEAD_DIM:]], axis=1)

    v = jnp.dot(u, wv_ref[...], preferred_element_type=F32).astype(BF16)
    lane = lax.broadcasted_iota(jnp.int32, (v.shape[0], AT_HEAD_DIM), 1)
    tail = jnp.where(lane == 0, 1.0, 0.0).astype(BF16)
    av_ref[0] = jnp.concatenate([v[:, :AT_HEAD_DIM], tail, v[:, AT_HEAD_DIM:], tail], axis=1)


def _rope_tables(seq):
    f = ROPE_AXIS_DIM // 2
    t = jnp.arange(seq)
    r = (t // GRID_W).astype(F32)
    c = (t % GRID_W).astype(F32)
    freqs = ROPE_THETA ** (-jnp.arange(f, dtype=F32) / f)
    ang = jnp.concatenate([r[:, None] * freqs, c[:, None] * freqs], axis=-1)
    ang = jnp.concatenate([jnp.zeros((PAD_ROWS, 2 * f), F32), ang], axis=0)
    cos, sin = jnp.cos(ang), jnp.sin(ang)
    cos_h = jnp.concatenate([cos[:, :f], cos[:, :f], cos[:, f:], cos[:, f:]], axis=1)
    sin_h = jnp.concatenate([-sin[:, :f], sin[:, :f], -sin[:, f:], sin[:, f:]], axis=1)
    return jnp.concatenate([cos_h, cos_h], axis=1), jnp.concatenate([sin_h, sin_h], axis=1)


def _in_projection(x, meta_tokens, w_in, norm_pre, q_norm, k_norm):
    bsz, seq, _ = x.shape
    lp = PAD_ROWS + seq
    nt = lp // ROW_TILE
    o = np.cumsum([0, DN_WIDTH * 3, DN_WIDTH, 2 * DN_HEADS, 2 * DN_HEADS, AT_WIDTH, AT_KV_WIDTH, AT_KV_WIDTH])
    wb = w_in.astype(BF16)
    w_qkv, w_z = wb[:, o[0]:o[1]], wb[:, o[1]:o[2]]
    w_ba = jnp.pad(wb[:, o[2]:o[4]], ((0, 0), (0, GATE_W - 4 * DN_HEADS)))
    w_q, w_k, w_v = wb[:, o[4]:o[5]], wb[:, o[5]:o[6]], wb[:, o[6]:o[7]]
    metap = jnp.pad(meta_tokens.astype(F32), ((PAD_ROWS - N_META, 0), (0, 0)))
    cos2, sin2 = _rope_tables(seq)
    qn = jnp.tile(q_norm.astype(F32), AT_Q_HEADS)[None]
    kn = jnp.tile(k_norm.astype(F32), AT_KV_HEADS)[None]

    def full(a):
        return pl.BlockSpec(a.shape, lambda b, i: (0,) * a.ndim)

    def rows(width):
        return pl.BlockSpec((1, ROW_TILE, width), lambda b, i: (b, i, 0))

    wpre = norm_pre.astype(F32)[None]
    ones_q = _block_diag_ones(AT_WIDTH, AT_HEAD_DIM)
    ones_k = _block_diag_ones(AT_KV_WIDTH, AT_HEAD_DIM)
    consts = (metap, wpre, w_qkv, w_z, w_ba, w_q, w_k, w_v, qn, kn)
    widths = (3 * DN_WIDTH, DN_WIDTH, GATE_W, AT_WIDTH, 2 * LANES, 2 * LANES)
    dtypes = (F32, F32, F32, BF16, BF16, BF16)
    return pl.pallas_call(
        _inproj_kernel,
        grid=(bsz, nt),
        in_specs=[pl.BlockSpec((1, ROW_TILE, D_MODEL), lambda b, i: (b, jnp.maximum(i - 1, 0), 0))]
                 + [full(a) for a in consts]
                 + [pl.BlockSpec((ROW_TILE, LANES), lambda b, i: (i, 0))] * 2
                 + [full(ones_q), full(ones_k)],
        out_specs=[rows(w) for w in widths],
        out_shape=[jax.ShapeDtypeStruct((bsz, lp, w), d) for w, d in zip(widths, dtypes)],
        compiler_params=_cparams(("parallel", "parallel")),
        name="in_projection",
    )(x, *consts, cos2, sin2, ones_q, ones_k)


HALO = 8


def _softplus(x):
    return jnp.maximum(x, 0.0) + jnp.log1p(jnp.exp(-jnp.abs(x)))


def _conv_kernel(x_ref, prev_ref, next_ref, ba_ref, cw_ref, alog_ref, dtb_ref, ones_ref,
                 q_ref, k_ref, v_ref, g_ref, xe_ref):
    i = pl.program_id(1)
    last = pl.num_programs(1) - 1
    t = ROW_TILE
    xe_ref[0:HALO] = jnp.where(i == 0, 0.0, prev_ref[0])
    xe_ref[HALO:HALO + t] = x_ref[0]
    xe_ref[HALO + t:2 * HALO + t] = jnp.where(i == last, 0.0, next_ref[0])
    acc = None
    for j in range(CONV_W):
        term = xe_ref[pl.ds(HALO - CONV_W // 2 + j, t), :] * cw_ref[j:j + 1, :]
        acc = term if acc is None else acc + term
    y = acc * _sigmoid(acc)

    row = lax.broadcasted_iota(jnp.int32, (t, 1), 0)
    valid = jnp.logical_or(i > 0, row >= PAD_ROWS - N_META)

    ones = ones_ref[...]
    q = y[:, :DN_WIDTH]
    k = y[:, DN_WIDTH:2 * DN_WIDTH]
    v = y[:, 2 * DN_WIDTH:]
    q = q * lax.rsqrt(_group_mean(q * q, ones) * DN_HEAD_DIM + EPS) * (DN_HEAD_DIM ** -0.5)
    k = k * lax.rsqrt(_group_mean(k * k, ones) * DN_HEAD_DIM + EPS)
    q_ref[0] = jnp.where(valid, q, 0.0)
    k_ref[0] = jnp.where(valid, k, 0.0)
    v_ref[0] = jnp.where(valid, v, 0.0)

    raw = ba_ref[0]
    lane = lax.broadcasted_iota(jnp.int32, raw.shape, 1)
    beta = _sigmoid(raw)
    g = -jnp.exp(alog_ref[...]) * _softplus(raw + dtb_ref[...])
    gates = jnp.where(lane < 2 * DN_HEADS, beta, jnp.where(lane < 4 * DN_HEADS, g, 0.0))
    g_ref[0] = jnp.where(valid, gates, 0.0)


def _conv_gates(qkv, ba, conv_w, a_log, dt_bias):
    bsz, lp, width = qkv.shape
    nt = lp // ROW_TILE
    per = ROW_TILE // HALO
    cw = jnp.pad(conv_w.astype(F32), ((0, 8 - CONV_W), (0, 0)))
    pad = (2 * DN_HEADS, GATE_W - 4 * DN_HEADS)
    alog = jnp.pad(a_log.astype(F32).reshape(-1), pad)[None]
    dtb = jnp.pad(dt_bias.astype(F32).reshape(-1), pad)[None]
    ones = _block_diag_ones(DN_WIDTH, DN_HEAD_DIM)

    def full(a):
        return pl.BlockSpec(a.shape, lambda b, i: (0,) * a.ndim)

    def rows(w):
        return pl.BlockSpec((1, ROW_TILE, w), lambda b, i: (b, i, 0))

    return pl.pallas_call(
        _conv_kernel,
        grid=(bsz, nt),
        in_specs=[rows(width),
                  pl.BlockSpec((1, HALO, width), lambda b, i: (b, jnp.maximum(i * per - 1, 0), 0)),
                  pl.BlockSpec((1, HALO, width), lambda b, i: (b, jnp.minimum((i + 1) * per, nt * per - 1), 0)),
                  rows(GATE_W), full(cw), full(alog), full(dtb), full(ones)],
        out_specs=[rows(DN_WIDTH)] * 3 + [rows(GATE_W)],
        out_shape=[jax.ShapeDtypeStruct((bsz, lp, DN_WIDTH), F32)] * 3
                  + [jax.ShapeDtypeStruct((bsz, lp, GATE_W), F32)],
        scratch_shapes=[pltpu.VMEM((ROW_TILE + 2 * HALO, width), F32)],
        compiler_params=_cparams(("parallel", "parallel")),
        name="conv_gates",
    )(qkv, qkv, qkv, ba, cw, alog, dtb, ones)


def _dot(a, b):
    return jnp.dot(a.astype(BF16), b.astype(BF16), preferred_element_type=F32)


def _dot_nt(a, b):
    return lax.dot_general(a.astype(BF16), b.astype(BF16), (((1,), (1,)), ((), ())),
                           preferred_element_type=F32)


INV_BLOCK = 32


def _unit_tri_inverses(neg_as, row, col):
    same = lambda w: (row // w) == (col // w)
    eye = jnp.where(row == col, 1.0, 0.0)
    xs = [jnp.where(same(INV_BLOCK), a, 0.0) for a in neg_as]
    ps = [eye + x for x in xs]
    for _ in range(int(math.log2(INV_BLOCK)) - 1):
        xs = [_dot(x, x) for x in xs]
        ps = [p + _dot(p, x) for p, x in zip(ps, xs)]
    w = INV_BLOCK
    while w < DN_CHUNK:
        band = jnp.logical_and(same(2 * w), jnp.logical_not(same(w)))
        ts = [_dot(p, jnp.where(band, a, 0.0)) for p, a in zip(ps, neg_as)]
        ps = [p + _dot(t, p) for p, t in zip(ps, ts)]
        w *= 2
    return [p - eye for p in ps]


def _delta_kernel(qf_ref, kf_ref, vf_ref, gf_ref, qb_ref, kb_ref, vb_ref, gb_ref,
                  of_ref, ob_ref, s_ref):
    s = pl.program_id(1)

    @pl.when(s == 0)
    def _():
        s_ref[...] = jnp.zeros_like(s_ref)

    c = DN_CHUNK
    hd = DN_HEAD_DIM
    row = lax.broadcasted_iota(jnp.int32, (c, c), 0)
    col = lax.broadcasted_iota(jnp.int32, (c, c), 1)
    dirs = ((qf_ref, kf_ref, vf_ref, gf_ref, of_ref), (qb_ref, kb_ref, vb_ref, gb_ref, ob_ref))
    streams = [(d, h) for d in range(2) for h in range(DN_HEADS)]

    incl, strict, gates, gc_all, dm_all = [], [], [], [], []
    for d in range(2):
        order = col - row if d == 0 else (col - row) * jnp.where(s == 0, 1, -1)
        incl.append(order <= 0)
        strict.append(order < 0)
        lmat = jnp.where(incl[d], 1.0, 0.0).astype(BF16)
        g = dirs[d][3][0]
        gates.append(g)
        g_hi, g_lo = _split_bf16(g)
        gc_all.append(jnp.dot(lmat, g_hi, preferred_element_type=F32)
                      + jnp.dot(lmat, g_lo, preferred_element_type=F32))
        parts = []
        for h in range(DN_HEADS):
            jg = 2 * DN_HEADS + d * DN_HEADS + h
            parts.extend(_split_bf16(jnp.where(strict[d], g[:, jg:jg + 1], 0.0)))
        dm = jnp.dot(lmat, jnp.concatenate(parts, axis=1), preferred_element_type=F32)
        dm_all.append([dm[:, (2 * h) * c:(2 * h + 1) * c] + dm[:, (2 * h + 1) * c:(2 * h + 2) * c]
                       for h in range(DN_HEADS)])

    beta, gc, gc_last, decay, egc, q, k, v = [], [], [], [], [], [], [], []
    for d, h in streams:
        jb = d * DN_HEADS + h
        jg = 2 * DN_HEADS + jb
        sl = slice(h * hd, (h + 1) * hd)
        beta.append(gates[d][:, jb:jb + 1])
        gcs = gc_all[d][:, jg:jg + 1]
        gc.append(gcs)
        gc_last.append(gcs[c - 1:c, :] if d == 0 else jnp.where(s == 0, gcs[c - 1:c, :], gcs[0:1, :]))
        decay.append(jnp.where(incl[d], jnp.exp(dm_all[d][h]), 0.0))
        egc.append(jnp.exp(gcs))
        q.append(dirs[d][0][0, :, sl])
        k.append(dirs[d][1][0, :, sl])
        v.append(dirs[d][2][0, :, sl])

    n = len(streams)
    qkk = [_dot_nt(jnp.concatenate([q[i], k[i]], axis=0), k[i]) for i in range(n)]
    qk = [qkk[i][:c] * decay[i] for i in range(n)]
    neg_a = [jnp.where(strict[streams[i][0]], -(qkk[i][c:] * decay[i] * beta[i]), 0.0) for i in range(n)]
    tn = _unit_tri_inverses(neg_a, row, col)
    rhs = [jnp.concatenate([v[i] * beta[i], k[i] * (beta[i] * egc[i])], axis=1) for i in range(n)]
    uw = [rhs[i] + _dot(tn[i], rhs[i]) for i in range(n)]
    state = [s_ref[i] for i in range(n)]
    wq = [_dot(jnp.concatenate([uw[i][:, hd:], q[i] * egc[i]], axis=0), state[i]) for i in range(n)]
    v_new = [uw[i][:, :hd] - wq[i][:c] for i in range(n)]
    o = [wq[i][c:] + _dot(qk[i], v_new[i]) for i in range(n)]
    for i, (d, h) in enumerate(streams):
        dirs[d][4][0, :, h * hd:(h + 1) * hd] = o[i]
    k_dec = [k[i] * jnp.exp(gc_last[i] - gc[i]) for i in range(n)]
    s_new = [state[i] * jnp.exp(gc_last[i]) + _dot(k_dec[i].T, v_new[i]) for i in range(n)]
    for i in range(n):
        s_ref[i] = s_new[i]


def _delta_rule(q, k, v, gates, seq):
    bsz, lp, _ = q.shape
    ns = 1 + seq // DN_CHUNK

    def fwd(b, s):
        return (b, META_CHUNK + s, 0)

    def bwd(b, s):
        return (b, jnp.where(s == 0, META_CHUNK, META_CHUNK + ns - s), 0)

    def spec(w, imap):
        return pl.BlockSpec((1, DN_CHUNK, w), imap)

    ins = [spec(DN_WIDTH, fwd)] * 3 + [spec(GATE_W, fwd)] + [spec(DN_WIDTH, bwd)] * 3 + [spec(GATE_W, bwd)]
    return pl.pallas_call(
        _delta_kernel,
        grid=(bsz, ns),
        in_specs=ins,
        out_specs=[spec(DN_WIDTH, fwd), spec(DN_WIDTH, bwd)],
        out_shape=[jax.ShapeDtypeStruct((bsz, lp, DN_WIDTH), F32)] * 2,
        scratch_shapes=[pltpu.VMEM((2 * DN_HEADS, DN_HEAD_DIM, DN_HEAD_DIM), F32)],
        compiler_params=_cparams(("parallel", "arbitrary")),
        name="delta_rule",
    )(q, k, v, gates, q, k, v, gates)


KEY_START = PAD_ROWS - LANES
GROUP = AT_Q_HEADS // AT_KV_HEADS


def _kv_tile(n_keys):
    units = n_keys // LANES
    best = max(u for u in range(1, 6) if units % u == 0)
    return best * LANES


def _attn_kernel(q_ref, k_ref, v_ref, o_ref, qm_ref, m_ref, acc_ref, s_ref, *, tk, nkb):
    tq = q_ref.shape[1]
    lane = lax.broadcasted_iota(jnp.int32, (tq, LANES), 1)
    for j in range(GROUP):
        pair = q_ref[0, :, (j // 2) * LANES:(j // 2 + 1) * LANES]
        keep = (lane >= AT_HEAD_DIM) if j % 2 else (lane < AT_HEAD_DIM)
        qm_ref[j] = jnp.where(keep, pair, jnp.zeros_like(pair))

    def block(ref, kb):
        start = KEY_START + kb * tk
        if not isinstance(kb, int):
            start = pl.multiple_of(start, LANES)
        return ref[0, pl.ds(start, tk), :]

    keys = functools.partial(block, k_ref)
    values = functools.partial(block, v_ref)

    def scores(j, k):
        return lax.dot_general(qm_ref[j], k, (((1,), (1,)), ((), ())), preferred_element_type=F32)

    def row_max(s):
        m = s[:, :LANES]
        for t in range(1, tk // LANES):
            m = jnp.maximum(m, s[:, t * LANES:(t + 1) * LANES])
        return jnp.max(m, axis=1, keepdims=True)

    def widen(m):
        return jnp.concatenate([m] * (tk // LANES), axis=1)

    def unit(kb, j, first):
        if j + 1 < GROUP:
            nxt_kb = kb
        elif isinstance(kb, int):
            nxt_kb = min(kb + 1, nkb - 1)
        else:
            nxt_kb = jnp.minimum(kb + 1, nkb - 1)
        s_ref[(j + 1) % 2] = scores((j + 1) % GROUP, keys(nxt_kb))
        s = s_ref[j % 2]
        v = values(kb)
        if first:
            is_key = lax.broadcasted_iota(jnp.int32, (tq, tk), 1) >= PAD_ROWS - N_META - KEY_START
            s = jnp.where(is_key, s, -jnp.inf)
            m_new = jnp.broadcast_to(row_max(s), (tq, LANES))
            p = jnp.exp2(s - widen(m_new))
            acc_ref[j] = jnp.dot(p.astype(BF16), v, preferred_element_type=F32)
        else:
            m_prev = m_ref[j]
            m_new = jnp.maximum(m_prev, row_max(s))
            alpha = jnp.exp2(m_prev - m_new)
            p = jnp.exp2(s - widen(m_new))
            acc_ref[j] = alpha * acc_ref[j] + jnp.dot(p.astype(BF16), v, preferred_element_type=F32)
        m_ref[j] = m_new

    s_ref[0] = scores(0, keys(0))
    for j in range(GROUP):
        unit(0, j, True)

    def body(kb, carry):
        for j in range(GROUP):
            unit(kb, j, False)
        return carry

    lax.fori_loop(1, nkb, body, 0)

    for j in range(GROUP):
        acc = acc_ref[j]
        inv = 1.0 / acc[:, AT_HEAD_DIM:AT_HEAD_DIM + 1]
        o_ref[0, :, j * AT_HEAD_DIM:(j + 1) * AT_HEAD_DIM] = (acc[:, :AT_HEAD_DIM] * inv).astype(o_ref.dtype)


def _attention(aq, ak, av, seq):
    bsz, lp, _ = aq.shape
    tq = ROW_TILE
    n_keys = lp - KEY_START
    tk = _kv_tile(n_keys)
    gw = GROUP * AT_HEAD_DIM
    return pl.pallas_call(
        functools.partial(_attn_kernel, tk=tk, nkb=n_keys // tk),
        grid=(bsz, AT_KV_HEADS, seq // tq),
        in_specs=[pl.BlockSpec((1, tq, gw), lambda b, g, i: (b, i + PAD_ROWS // tq, g)),
                  pl.BlockSpec((1, lp, LANES), lambda b, g, i: (b, 0, g)),
                  pl.BlockSpec((1, lp, LANES), lambda b, g, i: (b, 0, g))],
        out_specs=pl.BlockSpec((1, tq, gw), lambda b, g, i: (b, i, g)),
        out_shape=jax.ShapeDtypeStruct((bsz, seq, AT_WIDTH), BF16),
        scratch_shapes=[pltpu.VMEM((GROUP, tq, LANES), BF16),
                        pltpu.VMEM((GROUP, tq, LANES), F32),
                        pltpu.VMEM((GROUP, tq, LANES), F32),
                        pltpu.VMEM((2, tq, tk), F32)],
        compiler_params=_cparams(("parallel", "parallel", "parallel")),
        name="gqa_attention",
    )(aq, ak, av)


FF_TILE = 1024


def _rms(x, w):
    return x * lax.rsqrt(jnp.mean(x * x, axis=-1, keepdims=True) + EPS) * w


def _out_mlp_kernel(x_ref, of_ref, ob_ref, z_ref, at_ref, dnw_ref, ones_ref, wod_ref, woa_ref, npost_ref,
                    nmpre_ref, wup_ref, wdn_ref, nmpost_ref, out_ref):
    o = of_ref[0] + ob_ref[0]
    o = o * lax.rsqrt(_group_mean(o * o, ones_ref[...]) + EPS) * dnw_ref[...]
    z = z_ref[0]
    dn = (o * (z * _sigmoid(z))).astype(BF16)
    mix = (jnp.dot(dn, wod_ref[...], preferred_element_type=F32)
           + jnp.dot(at_ref[0], woa_ref[...], preferred_element_type=F32))
    h1 = x_ref[0] + _rms(mix, npost_ref[...])
    u = _rms(h1, nmpre_ref[...]).astype(BF16)
    def up(j):
        return jnp.dot(u, wup_ref[:, j * FF_TILE:(j + 1) * FF_TILE], preferred_element_type=F32)

    n_ff = D_FF // FF_TILE
    f = None
    a = up(0)
    for j in range(n_ff):
        a_next = up(j + 1) if j + 1 < n_ff else None
        a = jnp.maximum(a, 0.0)
        part = jnp.dot((a * a).astype(BF16), wdn_ref[j * FF_TILE:(j + 1) * FF_TILE, :],
                       preferred_element_type=F32)
        f = part if f is None else f + part
        a = a_next
    out_ref[0] = h1 + _rms(f, nmpost_ref[...])


def _out_mlp(x, o_f, o_b, z, at, dn_out_norm, w_out, norm_mix_post, norm_mlp_pre, w_up, w_down, norm_mlp_post):
    bsz, seq, _ = x.shape
    off = PAD_ROWS // ROW_TILE
    dnw = jnp.tile(dn_out_norm.astype(F32), DN_HEADS)[None]
    ones = _block_diag_ones(DN_WIDTH, DN_HEAD_DIM)
    wo = w_out.astype(BF16)
    consts = (dnw, ones, wo[:DN_WIDTH], wo[DN_WIDTH:], norm_mix_post.astype(F32)[None],
              norm_mlp_pre.astype(F32)[None], w_up.astype(BF16), w_down.astype(BF16),
              norm_mlp_post.astype(F32)[None])

    def full(a):
        return pl.BlockSpec(a.shape, lambda b, i: (0,) * a.ndim, pipeline_mode=pl.Buffered(1))

    def real(w):
        return pl.BlockSpec((1, ROW_TILE, w), lambda b, i: (b, i, 0))

    def padded(w):
        return pl.BlockSpec((1, ROW_TILE, w), lambda b, i: (b, i + off, 0))

    return pl.pallas_call(
        _out_mlp_kernel,
        grid=(bsz, seq // ROW_TILE),
        in_specs=[real(D_MODEL), padded(DN_WIDTH), padded(DN_WIDTH), padded(DN_WIDTH), real(AT_WIDTH)]
                 + [full(a) for a in consts],
        out_specs=real(D_MODEL),
        out_shape=jax.ShapeDtypeStruct((bsz, seq, D_MODEL), x.dtype),
        compiler_params=_cparams(("parallel", "parallel")),
        name="out_mlp",
    )(x, o_f, o_b, z, at, *consts)


def kernel(x, meta_tokens, w_in, conv_w, a_log, dt_bias, dn_out_norm, q_norm, k_norm, w_out,
           norm_mix_pre, norm_mix_post, w_up, w_down, norm_mlp_pre, norm_mlp_post):
    assert w_in.shape[0] == 1, "single-layer block: meta-token outputs are not materialised"
    bsz, seq, d = x.shape
    assert d == D_MODEL and seq % ROW_TILE == 0 and seq % GRID_W == 0
    qkv, z, ba, aq, ak, av = _in_projection(x, meta_tokens, w_in[0], norm_mix_pre[0], q_norm[0], k_norm[0])
    dq, dk, dv, gates = _conv_gates(qkv, ba, conv_w[0], a_log[0], dt_bias[0])
    o_f, o_b = _delta_rule(dq, dk, dv, gates, seq)
    at = _attention(aq, ak, av, seq)
    return _out_mlp(x, o_f, o_b, z, at, dn_out_norm[0], w_out[0], norm_mix_post[0], norm_mlp_pre[0],
                    w_up[0], w_down[0], norm_mlp_post[0])
```

```python
import functools
import math

import numpy as np
import jax
import jax.numpy as jnp
from jax import lax
from jax.experimental import pallas as pl
from jax.experimental.pallas import tpu as pltpu

F32 = jnp.float32
BF16 = jnp.bfloat16

D_MODEL = 1024
N_META = 16
GRID_W = 64
DN_HEADS = 4
DN_HEAD_DIM = 128
DN_WIDTH = DN_HEADS * DN_HEAD_DIM
CONV_W = 5
AT_Q_HEADS = 8
AT_KV_HEADS = 2
AT_HEAD_DIM = 64
AT_WIDTH = AT_Q_HEADS * AT_HEAD_DIM
AT_KV_WIDTH = AT_KV_HEADS * AT_HEAD_DIM
ROPE_THETA = 10000.0
ROPE_AXIS_DIM = AT_HEAD_DIM // 2
D_FF = 4 * D_MODEL
EPS = 1e-6

LANES = 128
ROW_TILE = 512
PAD_ROWS = ROW_TILE
DN_CHUNK = 128
META_CHUNK = PAD_ROWS // DN_CHUNK - 1
GATE_W = LANES
VMEM_LIMIT = 56 * 1024 * 1024


def _cparams(sem):
    return pltpu.CompilerParams(dimension_semantics=sem, vmem_limit_bytes=VMEM_LIMIT)


def _split_bf16(x):
    hi = x.astype(BF16)
    lo = (x - hi.astype(F32)).astype(BF16)
    return hi, lo


def _group_mean(sq, ones_bd):
    return jnp.dot(sq.astype(BF16), ones_bd, preferred_element_type=F32)


def _block_diag_ones(width, group):
    idx = np.arange(width) // group
    return jnp.asarray((idx[:, None] == idx[None, :]).astype(np.float32) / group, dtype=BF16)


def _sigmoid(x):
    return 1.0 / (1.0 + jnp.exp(-x))


def _rope(x, cos, sin_signed, width):
    half = ROPE_AXIS_DIM // 2
    lane = lax.broadcasted_iota(jnp.int32, x.shape, 1)
    partner = jnp.where((lane % ROPE_AXIS_DIM) < half,
                        pltpu.roll(x, width - half, 1),
                        pltpu.roll(x, half, 1))
    return x * cos + partner * sin_signed


def _inproj_kernel(x_ref, metap_ref, wpre_ref, wqkv_ref, wz_ref, wba_ref, wq_ref, wk_ref, wv_ref,
                   qn_ref, kn_ref, cos_ref, sin_ref, onesq_ref, onesk_ref,
                   qkv_ref, z_ref, ba_ref, aq_ref, ak_ref, av_ref):
    i = pl.program_id(1)
    h = jnp.where(i == 0, metap_ref[...], x_ref[0])
    ms = jnp.mean(h * h, axis=-1, keepdims=True)
    u = (h * lax.rsqrt(ms + EPS) * wpre_ref[...]).astype(BF16)

    qkv_ref[0] = jnp.dot(u, wqkv_ref[...], preferred_element_type=F32)
    z_ref[0] = jnp.dot(u, wz_ref[...], preferred_element_type=F32)
    ba_ref[0] = jnp.dot(u, wba_ref[...], preferred_element_type=F32)

    cos2 = cos_ref[...]
    sin2 = sin_ref[...]
    q = jnp.dot(u, wq_ref[...], preferred_element_type=F32)
    q = q * lax.rsqrt(_group_mean(q * q, onesq_ref[...]) + EPS) * qn_ref[...]
    reps = AT_WIDTH // LANES
    q = _rope(q, jnp.concatenate([cos2] * reps, axis=1), jnp.concatenate([sin2] * reps, axis=1), AT_WIDTH)
    aq_ref[0] = (q * (AT_HEAD_DIM ** -0.5 * math.log2(math.e))).T.astype(BF16)

    k = jnp.dot(u, wk_ref[...], preferred_element_type=F32)
    k = k * lax.rsqrt(_group_mean(k * k, onesk_ref[...]) + EPS) * kn_ref[...]
    k = _rope(k, cos2, sin2, AT_KV_WIDTH).astype(BF16)
    ak_ref[0] = jnp.concatenate([k[:, :AT_HEAD_DIM], k[:, :AT_HEAD_DIM],
                                 k[:, AT_HEAD_DIM:], k[:, AT_HEAD_DIM:]], axis=1)

    v = jnp.dot(u, wv_ref[...], preferred_element_type=F32)
    lane = lax.broadcasted_iota(jnp.int32, (v.shape[0], AT_HEAD_DIM), 1)
    tail = jnp.where(lane == 0, 1.0, 0.0)
    av_ref[0] = jnp.concatenate([v[:, :AT_HEAD_DIM], tail, v[:, AT_HEAD_DIM:], tail], axis=1).T.astype(BF16)


def _rope_tables(seq):
    f = ROPE_AXIS_DIM // 2
    t = jnp.arange(seq)
    r = (t // GRID_W).astype(F32)
    c = (t % GRID_W).astype(F32)
    freqs = ROPE_THETA ** (-jnp.arange(f, dtype=F32) / f)
    ang = jnp.concatenate([r[:, None] * freqs, c[:, None] * freqs], axis=-1)
    ang = jnp.concatenate([jnp.zeros((PAD_ROWS, 2 * f), F32), ang], axis=0)
    cos, sin = jnp.cos(ang), jnp.sin(ang)
    cos_h = jnp.concatenate([cos[:, :f], cos[:, :f], cos[:, f:], cos[:, f:]], axis=1)
    sin_h = jnp.concatenate([-sin[:, :f], sin[:, :f], -sin[:, f:], sin[:, f:]], axis=1)
    return jnp.concatenate([cos_h, cos_h], axis=1), jnp.concatenate([sin_h, sin_h], axis=1)


def _in_projection(x, meta_tokens, w_in, norm_pre, q_norm, k_norm):
    bsz, seq, _ = x.shape
    lp = PAD_ROWS + seq
    nt = lp // ROW_TILE
    o = np.cumsum([0, DN_WIDTH * 3, DN_WIDTH, 2 * DN_HEADS, 2 * DN_HEADS, AT_WIDTH, AT_KV_WIDTH, AT_KV_WIDTH])
    wb = w_in.astype(BF16)
    w_qkv, w_z = wb[:, o[0]:o[1]], wb[:, o[1]:o[2]]
    w_ba = jnp.pad(wb[:, o[2]:o[4]], ((0, 0), (0, GATE_W - 4 * DN_HEADS)))
    w_q, w_k, w_v = wb[:, o[4]:o[5]], wb[:, o[5]:o[6]], wb[:, o[6]:o[7]]
    metap = jnp.pad(meta_tokens.astype(F32), ((PAD_ROWS - N_META, 0), (0, 0)))
    cos2, sin2 = _rope_tables(seq)
    qn = jnp.tile(q_norm.astype(F32), AT_Q_HEADS)[None]
    kn = jnp.tile(k_norm.astype(F32), AT_KV_HEADS)[None]

    def full(a):
        return pl.BlockSpec(a.shape, lambda b, i: (0,) * a.ndim)

    def rows(width):
        return pl.BlockSpec((1, ROW_TILE, width), lambda b, i: (b, i, 0))

    wpre = norm_pre.astype(F32)[None]
    ones_q = _block_diag_ones(AT_WIDTH, AT_HEAD_DIM)
    ones_k = _block_diag_ones(AT_KV_WIDTH, AT_HEAD_DIM)
    consts = (metap, wpre, w_qkv, w_z, w_ba, w_q, w_k, w_v, qn, kn)

    def cols(height):
        return pl.BlockSpec((1, height, ROW_TILE), lambda b, i: (b, 0, i))

    q_spec = pl.BlockSpec((1, AT_WIDTH, ROW_TILE), lambda b, i: (b, 0, jnp.maximum(i - 1, 0)))
    out_specs = [rows(3 * DN_WIDTH), rows(DN_WIDTH), rows(GATE_W), q_spec, rows(2 * LANES), cols(2 * LANES)]
    out_shape = [jax.ShapeDtypeStruct((bsz, lp, 3 * DN_WIDTH), F32),
                 jax.ShapeDtypeStruct((bsz, lp, DN_WIDTH), F32),
                 jax.ShapeDtypeStruct((bsz, lp, GATE_W), F32),
                 jax.ShapeDtypeStruct((bsz, AT_WIDTH, seq), BF16),
                 jax.ShapeDtypeStruct((bsz, lp, 2 * LANES), BF16),
                 jax.ShapeDtypeStruct((bsz, 2 * LANES, lp), BF16)]
    return pl.pallas_call(
        _inproj_kernel,
        grid=(bsz, nt),
        in_specs=[pl.BlockSpec((1, ROW_TILE, D_MODEL), lambda b, i: (b, jnp.maximum(i - 1, 0), 0))]
                 + [full(a) for a in consts]
                 + [pl.BlockSpec((ROW_TILE, LANES), lambda b, i: (i, 0))] * 2
                 + [full(ones_q), full(ones_k)],
        out_specs=out_specs,
        out_shape=out_shape,
        compiler_params=_cparams(("parallel", "arbitrary")),
        name="in_projection",
    )(x, *consts, cos2, sin2, ones_q, ones_k)


HALO = 8


def _softplus(x):
    return jnp.maximum(x, 0.0) + jnp.log1p(jnp.exp(-jnp.abs(x)))


def _conv_kernel(x_ref, prev_ref, next_ref, ba_ref, cw_ref, alog_ref, dtb_ref, ones_ref,
                 q_ref, k_ref, v_ref, g_ref, xe_ref):
    i = pl.program_id(1)
    last = pl.num_programs(1) - 1
    t = ROW_TILE
    xe_ref[0:HALO] = jnp.where(i == 0, 0.0, prev_ref[0])
    xe_ref[HALO:HALO + t] = x_ref[0]
    xe_ref[HALO + t:2 * HALO + t] = jnp.where(i == last, 0.0, next_ref[0])
    acc = None
    for j in range(CONV_W):
        term = xe_ref[pl.ds(HALO - CONV_W // 2 + j, t), :] * cw_ref[j:j + 1, :]
        acc = term if acc is None else acc + term
    y = acc * _sigmoid(acc)

    row = lax.broadcasted_iota(jnp.int32, (t, 1), 0)
    valid = jnp.logical_or(i > 0, row >= PAD_ROWS - N_META)

    ones = ones_ref[...]
    q = y[:, :DN_WIDTH]
    k = y[:, DN_WIDTH:2 * DN_WIDTH]
    v = y[:, 2 * DN_WIDTH:]
    q = q * lax.rsqrt(_group_mean(q * q, ones) * DN_HEAD_DIM + EPS) * (DN_HEAD_DIM ** -0.5)
    k = k * lax.rsqrt(_group_mean(k * k, ones) * DN_HEAD_DIM + EPS)
    q_ref[0] = jnp.where(valid, q, 0.0)
    k_ref[0] = jnp.where(valid, k, 0.0)
    v_ref[0] = jnp.where(valid, v, 0.0)

    raw = ba_ref[0]
    lane = lax.broadcasted_iota(jnp.int32, raw.shape, 1)
    beta = _sigmoid(raw)
    g = -jnp.exp(alog_ref[...]) * _softplus(raw + dtb_ref[...])
    gates = jnp.where(lane < 2 * DN_HEADS, beta, jnp.where(lane < 4 * DN_HEADS, g, 0.0))
    g_ref[0] = jnp.where(valid, gates, 0.0)


def _conv_gates(qkv, ba, conv_w, a_log, dt_bias):
    bsz, lp, width = qkv.shape
    nt = lp // ROW_TILE
    per = ROW_TILE // HALO
    cw = jnp.pad(conv_w.astype(F32), ((0, 8 - CONV_W), (0, 0)))
    pad = (2 * DN_HEADS, GATE_W - 4 * DN_HEADS)
    alog = jnp.pad(a_log.astype(F32).reshape(-1), pad)[None]
    dtb = jnp.pad(dt_bias.astype(F32).reshape(-1), pad)[None]
    ones = _block_diag_ones(DN_WIDTH, DN_HEAD_DIM)

    def full(a):
        return pl.BlockSpec(a.shape, lambda b, i: (0,) * a.ndim)

    def rows(w):
        return pl.BlockSpec((1, ROW_TILE, w), lambda b, i: (b, i, 0))

    return pl.pallas_call(
        _conv_kernel,
        grid=(bsz, nt),
        in_specs=[rows(width),
                  pl.BlockSpec((1, HALO, width), lambda b, i: (b, jnp.maximum(i * per - 1, 0), 0)),
                  pl.BlockSpec((1, HALO, width), lambda b, i: (b, jnp.minimum((i + 1) * per, nt * per - 1), 0)),
                  rows(GATE_W), full(cw), full(alog), full(dtb), full(ones)],
        out_specs=[rows(DN_WIDTH)] * 3 + [rows(GATE_W)],
        out_shape=[jax.ShapeDtypeStruct((bsz, lp, DN_WIDTH), F32)] * 3
                  + [jax.ShapeDtypeStruct((bsz, lp, GATE_W), F32)],
        scratch_shapes=[pltpu.VMEM((ROW_TILE + 2 * HALO, width), F32)],
        compiler_params=_cparams(("parallel", "parallel")),
        name="conv_gates",
    )(qkv, qkv, qkv, ba, cw, alog, dtb, ones)


def _dot(a, b):
    return jnp.dot(a.astype(BF16), b.astype(BF16), preferred_element_type=F32)


def _dot_nt(a, b):
    return lax.dot_general(a.astype(BF16), b.astype(BF16), (((1,), (1,)), ((), ())),
                           preferred_element_type=F32)


INV_BLOCK = 32


def _unit_tri_inverses(neg_as, row, col):
    same = lambda w: (row // w) == (col // w)
    eye = jnp.where(row == col, 1.0, 0.0)
    xs = [jnp.where(same(INV_BLOCK), a, 0.0) for a in neg_as]
    ps = [eye + x for x in xs]
    for _ in range(int(math.log2(INV_BLOCK)) - 1):
        xs = [_dot(x, x) for x in xs]
        ps = [p + _dot(p, x) for p, x in zip(ps, xs)]
    w = INV_BLOCK
    while w < DN_CHUNK:
        band = jnp.logical_and(same(2 * w), jnp.logical_not(same(w)))
        ts = [_dot(p, jnp.where(band, a, 0.0)) for p, a in zip(ps, neg_as)]
        ps = [p + _dot(t, p) for p, t in zip(ps, ts)]
        w *= 2
    return [p - eye for p in ps]


def _delta_kernel(qf_ref, kf_ref, vf_ref, gf_ref, qb_ref, kb_ref, vb_ref, gb_ref,
                  of_ref, ob_ref, s_ref):
    s = pl.program_id(1)

    @pl.when(s == 0)
    def _():
        s_ref[...] = jnp.zeros_like(s_ref)

    c = DN_CHUNK
    hd = DN_HEAD_DIM
    nb = qf_ref.shape[0]
    row = lax.broadcasted_iota(jnp.int32, (c, c), 0)
    col = lax.broadcasted_iota(jnp.int32, (c, c), 1)
    dirs = ((qf_ref, kf_ref, vf_ref, gf_ref, of_ref), (qb_ref, kb_ref, vb_ref, gb_ref, ob_ref))
    scans = [(bi, d) for bi in range(nb) for d in range(2)]
    streams = [(e, h) for e in range(len(scans)) for h in range(DN_HEADS)]

    incl, strict, lmat = [], [], []
    for d in range(2):
        order = col - row if d == 0 else (col - row) * jnp.where(s == 0, 1, -1)
        incl.append(order <= 0)
        strict.append(order < 0)
        lmat.append(jnp.where(incl[d], 1.0, 0.0).astype(BF16))

    gates, gc_all, dm_all = [], [], []
    for bi, d in scans:
        g = dirs[d][3][bi]
        gates.append(g)
        g_hi, g_lo = _split_bf16(g)
        gc_all.append(jnp.dot(lmat[d], g_hi, preferred_element_type=F32)
                      + jnp.dot(lmat[d], g_lo, preferred_element_type=F32))
        parts = []
        for h in range(DN_HEADS):
            jg = 2 * DN_HEADS + d * DN_HEADS + h
            parts.extend(_split_bf16(jnp.where(strict[d], g[:, jg:jg + 1], 0.0)))
        dm = jnp.dot(lmat[d], jnp.concatenate(parts, axis=1), preferred_element_type=F32)
        dm_all.append([dm[:, (2 * h) * c:(2 * h + 1) * c] + dm[:, (2 * h + 1) * c:(2 * h + 2) * c]
                       for h in range(DN_HEADS)])

    beta, gc, gc_last, decay, egc, q, k, v = [], [], [], [], [], [], [], []
    for e, h in streams:
        bi, d = scans[e]
        jb = d * DN_HEADS + h
        jg = 2 * DN_HEADS + jb
        sl = slice(h * hd, (h + 1) * hd)
        beta.append(gates[e][:, jb:jb + 1])
        gcs = gc_all[e][:, jg:jg + 1]
        gc.append(gcs)
        gc_last.append(gcs[c - 1:c, :] if d == 0 else jnp.where(s == 0, gcs[c - 1:c, :], gcs[0:1, :]))
        decay.append(jnp.where(incl[d], jnp.exp(dm_all[e][h]), 0.0))
        egc.append(jnp.exp(gcs))
        q.append(dirs[d][0][bi, :, sl])
        k.append(dirs[d][1][bi, :, sl])
        v.append(dirs[d][2][bi, :, sl])

    n = len(streams)
    tri = [strict[scans[e][1]] for e, _ in streams]
    qkk = [_dot_nt(jnp.concatenate([q[i], k[i]], axis=0), k[i]) for i in range(n)]
    qk = [qkk[i][:c] * decay[i] for i in range(n)]
    neg_a = [jnp.where(tri[i], -(qkk[i][c:] * decay[i] * beta[i]), 0.0) for i in range(n)]
    tn = _unit_tri_inverses(neg_a, row, col)
    rhs = [jnp.concatenate([v[i] * beta[i], k[i] * (beta[i] * egc[i])], axis=1) for i in range(n)]
    uw = [rhs[i] + _dot(tn[i], rhs[i]) for i in range(n)]
    state = [s_ref[i] for i in range(n)]
    wq = [_dot(jnp.concatenate([uw[i][:, hd:], q[i] * egc[i]], axis=0), state[i]) for i in range(n)]
    v_new = [uw[i][:, :hd] - wq[i][:c] for i in range(n)]
    o = [wq[i][c:] + _dot(qk[i], v_new[i]) for i in range(n)]
    for i, (e, h) in enumerate(streams):
        bi, d = scans[e]
        dirs[d][4][bi, :, h * hd:(h + 1) * hd] = o[i]
    k_dec = [k[i] * jnp.exp(gc_last[i] - gc[i]) for i in range(n)]
    s_new = [state[i] * jnp.exp(gc_last[i]) + _dot(k_dec[i].T, v_new[i]) for i in range(n)]
    for i in range(n):
        s_ref[i] = s_new[i]


DN_BATCH = 2


def _delta_rule(q, k, v, gates, seq):
    bsz, lp, _ = q.shape
    ns = 1 + seq // DN_CHUNK
    nb = math.gcd(DN_BATCH, bsz)

    def fwd(b, s):
        return (b, META_CHUNK + s, 0)

    def bwd(b, s):
        return (b, jnp.where(s == 0, META_CHUNK, META_CHUNK + ns - s), 0)

    def spec(w, imap):
        return pl.BlockSpec((nb, DN_CHUNK, w), imap)

    ins = [spec(DN_WIDTH, fwd)] * 3 + [spec(GATE_W, fwd)] + [spec(DN_WIDTH, bwd)] * 3 + [spec(GATE_W, bwd)]
    return pl.pallas_call(
        _delta_kernel,
        grid=(bsz // nb, ns),
        in_specs=ins,
        out_specs=[spec(DN_WIDTH, fwd), spec(DN_WIDTH, bwd)],
        out_shape=[jax.ShapeDtypeStruct((bsz, lp, DN_WIDTH), F32)] * 2,
        scratch_shapes=[pltpu.VMEM((nb * 2 * DN_HEADS, DN_HEAD_DIM, DN_HEAD_DIM), F32)],
        compiler_params=_cparams(("parallel", "arbitrary")),
        name="delta_rule",
    )(q, k, v, gates, q, k, v, gates)


MXU_DIM = 256
KEY_START = PAD_ROWS - MXU_DIM
GROUP = AT_Q_HEADS // AT_KV_HEADS
Q_TILE = 512


def _kv_tile(n_keys):
    units = n_keys // MXU_DIM
    best = max(u for u in range(1, 4) if units % u == 0)
    return best * MXU_DIM


SUBLANES = 8


def _attn_kernel(qt_ref, k_ref, vt_ref, o_ref, qp_ref, m_ref, acc_ref, s_ref, *, tk, nkb):
    tq = qt_ref.shape[2]
    hd = AT_HEAD_DIM
    for j in range(GROUP):
        qp_ref[j, :hd, :] = qt_ref[0, j * hd:(j + 1) * hd, :]
        qp_ref[j, hd:, :] = jnp.zeros((LANES - hd, tq), BF16)

    def start_of(kb):
        start = KEY_START + kb * tk
        return start if isinstance(kb, int) else pl.multiple_of(start, LANES)

    def scores(j, kb):
        return jnp.dot(k_ref[0, pl.ds(start_of(kb), tk), :], qp_ref[j], preferred_element_type=F32)

    def col_max(s):
        m = s[:SUBLANES]
        for t in range(1, tk // SUBLANES):
            m = jnp.maximum(m, s[t * SUBLANES:(t + 1) * SUBLANES])
        return jnp.broadcast_to(jnp.max(m, axis=0, keepdims=True), (SUBLANES, tq))

    def rows(m, n):
        return jnp.concatenate([m] * (n // SUBLANES), axis=0)

    def unit(kb, j, first):
        if j + 1 < GROUP:
            nxt_kb = kb
        elif isinstance(kb, int):
            nxt_kb = min(kb + 1, nkb - 1)
        else:
            nxt_kb = jnp.minimum(kb + 1, nkb - 1)
        s_ref[(j + 1) % 2] = scores((j + 1) % GROUP, nxt_kb)
        s = s_ref[j % 2]
        vt = vt_ref[0, :, pl.ds(start_of(kb), tk)]
        if first:
            is_key = lax.broadcasted_iota(jnp.int32, (tk, tq), 0) >= PAD_ROWS - N_META - KEY_START
            s = jnp.where(is_key, s, -jnp.inf)
            m_new = col_max(s)
            p = jnp.exp2(s - rows(m_new, tk))
            acc_ref[j] = jnp.dot(vt, p.astype(BF16), preferred_element_type=F32)
        else:
            m_prev = m_ref[j]
            m_new = jnp.maximum(m_prev, col_max(s))
            alpha = jnp.exp2(m_prev - m_new)
            p = jnp.exp2(s - rows(m_new, tk))
            acc_ref[j] = rows(alpha, LANES) * acc_ref[j] + jnp.dot(vt, p.astype(BF16), preferred_element_type=F32)
        m_ref[j] = m_new

    s_ref[0] = scores(0, 0)
    for j in range(GROUP):
        unit(0, j, True)

    def body(kb, carry):
        for j in range(GROUP):
            unit(kb, j, False)
        return carry

    lax.fori_loop(1, nkb, body, 0)

    outs = []
    for j in range(GROUP):
        acc = acc_ref[j]
        outs.append(acc[:hd] * (1.0 / acc[hd:hd + 1]))
    o_ref[0] = jnp.concatenate(outs, axis=0).T.astype(o_ref.dtype)


def _attention(aqt, ak, avt, seq):
    bsz, lp, _ = ak.shape
    tq = math.gcd(Q_TILE, seq)
    n_keys = lp - KEY_START
    tk = _kv_tile(n_keys)
    gw = GROUP * AT_HEAD_DIM
    assert GROUP % 2 == 0
    return pl.pallas_call(
        functools.partial(_attn_kernel, tk=tk, nkb=n_keys // tk),
        grid=(bsz, AT_KV_HEADS, seq // tq),
        in_specs=[pl.BlockSpec((1, gw, tq), lambda b, g, i: (b, g, i)),
                  pl.BlockSpec((1, lp, LANES), lambda b, g, i: (b, 0, g)),
                  pl.BlockSpec((1, LANES, lp), lambda b, g, i: (b, g, 0))],
        out_specs=pl.BlockSpec((1, tq, gw), lambda b, g, i: (b, i, g)),
        out_shape=jax.ShapeDtypeStruct((bsz, seq, AT_WIDTH), BF16),
        scratch_shapes=[pltpu.VMEM((GROUP, LANES, tq), BF16),
                        pltpu.VMEM((GROUP, SUBLANES, tq), F32),
                        pltpu.VMEM((GROUP, LANES, tq), F32),
                        pltpu.VMEM((2, tk, tq), F32)],
        compiler_params=_cparams(("parallel", "parallel", "parallel")),
        name="gqa_attention",
    )(aqt, ak, avt)


FF_TILE = 1024


def _rms(x, w):
    return x * lax.rsqrt(jnp.mean(x * x, axis=-1, keepdims=True) + EPS) * w


def _out_mlp_kernel(x_ref, of_ref, ob_ref, z_ref, at_ref, dnw_ref, ones_ref, wod_ref, woa_ref, npost_ref,
                    nmpre_ref, wup_ref, wdn_ref, nmpost_ref, out_ref):
    o = of_ref[0] + ob_ref[0]
    o = o * lax.rsqrt(_group_mean(o * o, ones_ref[...]) + EPS) * dnw_ref[...]
    z = z_ref[0]
    dn = (o * (z * _sigmoid(z))).astype(BF16)
    mix = (jnp.dot(dn, wod_ref[...], preferred_element_type=F32)
           + jnp.dot(at_ref[0], woa_ref[...], preferred_element_type=F32))
    h1 = x_ref[0] + _rms(mix, npost_ref[...])
    u = _rms(h1, nmpre_ref[...]).astype(BF16)
    def up(j):
        return jnp.dot(u, wup_ref[:, j * FF_TILE:(j + 1) * FF_TILE], preferred_element_type=F32)

    n_ff = D_FF // FF_TILE
    f = None
    a = up(0)
    for j in range(n_ff):
        a_next = up(j + 1) if j + 1 < n_ff else None
        a = jnp.maximum(a, 0.0)
        part = jnp.dot((a * a).astype(BF16), wdn_ref[j * FF_TILE:(j + 1) * FF_TILE, :],
                       preferred_element_type=F32)
        f = part if f is None else f + part
        a = a_next
    out_ref[0] = h1 + _rms(f, nmpost_ref[...])


def _out_mlp(x, o_f, o_b, z, at, dn_out_norm, w_out, norm_mix_post, norm_mlp_pre, w_up, w_down, norm_mlp_post):
    bsz, seq, _ = x.shape
    off = PAD_ROWS // ROW_TILE
    dnw = jnp.tile(dn_out_norm.astype(F32), DN_HEADS)[None]
    ones = _block_diag_ones(DN_WIDTH, DN_HEAD_DIM)
    wo = w_out.astype(BF16)
    consts = (dnw, ones, wo[:DN_WIDTH], wo[DN_WIDTH:], norm_mix_post.astype(F32)[None],
              norm_mlp_pre.astype(F32)[None], w_up.astype(BF16), w_down.astype(BF16),
              norm_mlp_post.astype(F32)[None])

    def full(a):
        return pl.BlockSpec(a.shape, lambda b, i: (0,) * a.ndim, pipeline_mode=pl.Buffered(1))

    def real(w):
        return pl.BlockSpec((1, ROW_TILE, w), lambda b, i: (b, i, 0))

    def padded(w):
        return pl.BlockSpec((1, ROW_TILE, w), lambda b, i: (b, i + off, 0))

    return pl.pallas_call(
        _out_mlp_kernel,
        grid=(bsz, seq // ROW_TILE),
        in_specs=[real(D_MODEL), padded(DN_WIDTH), padded(DN_WIDTH), padded(DN_WIDTH), real(AT_WIDTH)]
                 + [full(a) for a in consts],
        out_specs=real(D_MODEL),
        out_shape=jax.ShapeDtypeStruct((bsz, seq, D_MODEL), x.dtype),
        compiler_params=_cparams(("parallel", "parallel")),
        name="out_mlp",
    )(x, o_f, o_b, z, at, *consts)


def kernel(x, meta_tokens, w_in, conv_w, a_log, dt_bias, dn_out_norm, q_norm, k_norm, w_out,
           norm_mix_pre, norm_mix_post, w_up, w_down, norm_mlp_pre, norm_mlp_post):
    assert w_in.shape[0] == 1, "single-layer block: meta-token outputs are not materialised"
    bsz, seq, d = x.shape
    assert d == D_MODEL and seq % ROW_TILE == 0 and seq % GRID_W == 0
    qkv, z, ba, aq, ak, av = _in_projection(x, meta_tokens, w_in[0], norm_mix_pre[0], q_norm[0], k_norm[0])
    dq, dk, dv, gates = _conv_gates(qkv, ba, conv_w[0], a_log[0], dt_bias[0])
    o_f, o_b = _delta_rule(dq, dk, dv, gates, seq)
    at = _attention(aq, ak, av, seq)
    return _out_mlp(x, o_f, o_b, z, at, dn_out_norm[0], w_out[0], norm_mix_post[0], norm_mlp_pre[0],
                    w_up[0], w_down[0], norm_mlp_post[0])
```

```python
import functools
import math

import numpy as np
import jax
import jax.numpy as jnp
from jax import lax
from jax.experimental import pallas as pl
from jax.experimental.pallas import tpu as pltpu

F32 = jnp.float32
BF16 = jnp.bfloat16

D_MODEL = 1024
N_META = 16
GRID_W = 64
DN_HEADS = 4
DN_HEAD_DIM = 128
DN_WIDTH = DN_HEADS * DN_HEAD_DIM
CONV_W = 5
AT_Q_HEADS = 8
AT_KV_HEADS = 2
AT_HEAD_DIM = 64
AT_WIDTH = AT_Q_HEADS * AT_HEAD_DIM
AT_KV_WIDTH = AT_KV_HEADS * AT_HEAD_DIM
ROPE_THETA = 10000.0
ROPE_AXIS_DIM = AT_HEAD_DIM // 2
D_FF = 4 * D_MODEL
EPS = 1e-6

LANES = 128
ROW_TILE = 512
ROW_SPLIT = 2
PAD_ROWS = ROW_TILE
DN_CHUNK = 128
META_CHUNK = PAD_ROWS // DN_CHUNK - 1
GATE_W = LANES
VMEM_LIMIT = 56 * 1024 * 1024


def _cparams(sem):
    return pltpu.CompilerParams(dimension_semantics=sem, vmem_limit_bytes=VMEM_LIMIT)


def _split_bf16(x):
    hi = x.astype(BF16)
    lo = (x - hi.astype(F32)).astype(BF16)
    return hi, lo


def _group_mean(sq, ones_bd):
    return jnp.dot(sq.astype(BF16), ones_bd, preferred_element_type=F32)


def _block_diag_ones(width, group):
    idx = np.arange(width) // group
    return jnp.asarray((idx[:, None] == idx[None, :]).astype(np.float32) / group, dtype=BF16)


def _sigmoid(x):
    return 1.0 / (1.0 + jnp.exp(-x))


def _rope(x, cos, sin_signed, width):
    half = ROPE_AXIS_DIM // 2
    lane = lax.broadcasted_iota(jnp.int32, x.shape, 1)
    partner = jnp.where((lane % ROPE_AXIS_DIM) < half,
                        pltpu.roll(x, width - half, 1),
                        pltpu.roll(x, half, 1))
    return x * cos + partner * sin_signed


HALO = 8


def _softplus(x):
    return jnp.maximum(x, 0.0) + jnp.log1p(jnp.exp(-jnp.abs(x)))


def _inproj_kernel(x_ref, xprev_ref, xnext_ref, metap_ref, wpre_ref, wqkv_ref, wz_ref, wba_ref, wq_ref, wk_ref,
                   wv_ref, qn_ref, kn_ref, cw_ref, alog_ref, dtb_ref, cos_ref, sin_ref, onesq_ref, onesk_ref,
                   onesd_ref, dq_ref, dk_ref, dv_ref, g_ref, z_ref, aq_ref, ak_ref, av_ref):
    i = pl.program_id(1)
    last = pl.num_programs(1) - 1
    rows = ROW_TILE // ROW_SPLIT
    slabs = [slice(p * rows, (p + 1) * rows) for p in range(ROW_SPLIT)]

    hc = jnp.where(i == 0, metap_ref[...], x_ref[0])
    hp = jnp.where(i == 0, 0.0, jnp.where(i == 1, metap_ref[ROW_TILE - HALO:, :], xprev_ref[0]))
    hn = jnp.where(i == last, 0.0, xnext_ref[0])
    he = jnp.concatenate([hp, hc, hn], axis=0)
    ms = jnp.mean(he * he, axis=-1, keepdims=True)
    ue = he * lax.rsqrt(ms + EPS) * wpre_ref[...]

    ext = rows + 2 * HALO
    u = [ue[HALO + p * rows:HALO + (p + 1) * rows].astype(BF16) for p in range(ROW_SPLIT)]
    qkv = [jnp.dot(ue[p * rows:p * rows + ext].astype(BF16), wqkv_ref[...], preferred_element_type=F32)
           for p in range(ROW_SPLIT)]
    for p, sl in enumerate(slabs):
        z_ref[0, sl, :] = jnp.dot(u[p], wz_ref[...], preferred_element_type=F32)
    ba = [jnp.dot(u[p], wba_ref[...], preferred_element_type=F32) for p in range(ROW_SPLIT)]
    q = [jnp.dot(u[p], wq_ref[...], preferred_element_type=F32) for p in range(ROW_SPLIT)]
    k = [jnp.dot(u[p], wk_ref[...], preferred_element_type=F32) for p in range(ROW_SPLIT)]
    v = [jnp.dot(u[p], wv_ref[...], preferred_element_type=F32) for p in range(ROW_SPLIT)]

    for p, sl in enumerate(slabs):
        acc = None
        for j in range(CONV_W):
            shift = (CONV_W // 2 - j) % ext
            shifted = qkv[p] if shift == 0 else pltpu.roll(qkv[p], shift, 0)
            term = shifted[HALO:HALO + rows] * cw_ref[j:j + 1, :]
            acc = term if acc is None else acc + term
        y = acc * _sigmoid(acc)

        ones = onesd_ref[...]
        dq = y[:, :DN_WIDTH]
        dk = y[:, DN_WIDTH:2 * DN_WIDTH]
        dq = dq * lax.rsqrt(_group_mean(dq * dq, ones) * DN_HEAD_DIM + EPS) * (DN_HEAD_DIM ** -0.5)
        dk = dk * lax.rsqrt(_group_mean(dk * dk, ones) * DN_HEAD_DIM + EPS)
        dq_ref[0, sl, :] = dq
        dk_ref[0, sl, :] = dk
        dv_ref[0, sl, :] = y[:, 2 * DN_WIDTH:]

        row = lax.broadcasted_iota(jnp.int32, (rows, 1), 0) + p * rows
        valid = jnp.logical_or(i > 0, row >= PAD_ROWS - N_META)

        raw = ba[p]
        lane = lax.broadcasted_iota(jnp.int32, raw.shape, 1)
        g = -jnp.exp(alog_ref[...]) * _softplus(raw + dtb_ref[...])
        gates = jnp.where(lane < 2 * DN_HEADS, _sigmoid(raw), jnp.where(lane < 4 * DN_HEADS, g, 0.0))
        g_ref[0, sl, :] = jnp.where(valid, gates, 0.0)

    reps = AT_WIDTH // LANES
    lane = lax.broadcasted_iota(jnp.int32, (rows, AT_HEAD_DIM), 1)
    tail = jnp.where(lane == 0, 1.0, 0.0)
    for p, sl in enumerate(slabs):
        cos2 = cos_ref[sl, :]
        sin2 = sin_ref[sl, :]
        qs = q[p] * lax.rsqrt(_group_mean(q[p] * q[p], onesq_ref[...]) + EPS) * qn_ref[...]
        qs = _rope(qs, jnp.concatenate([cos2] * reps, axis=1), jnp.concatenate([sin2] * reps, axis=1), AT_WIDTH)
        aq_ref[0, :, sl] = (qs * (AT_HEAD_DIM ** -0.5 * math.log2(math.e))).T.astype(BF16)

        ks = k[p] * lax.rsqrt(_group_mean(k[p] * k[p], onesk_ref[...]) + EPS) * kn_ref[...]
        ks = _rope(ks, cos2, sin2, AT_KV_WIDTH).astype(BF16)
        ak_ref[0, sl, :] = jnp.concatenate([ks[:, :AT_HEAD_DIM], ks[:, :AT_HEAD_DIM],
                                            ks[:, AT_HEAD_DIM:], ks[:, AT_HEAD_DIM:]], axis=1)

        vs = v[p]
        av_ref[0, :, sl] = jnp.concatenate([vs[:, :AT_HEAD_DIM], tail, vs[:, AT_HEAD_DIM:], tail],
                                           axis=1).T.astype(BF16)


def _rope_tables(seq):
    f = ROPE_AXIS_DIM // 2
    n_rows = seq // GRID_W
    freqs = ROPE_THETA ** (-jnp.arange(f, dtype=F32) / f)
    ang_r = jnp.arange(n_rows, dtype=F32)[:, None] * freqs
    ang_c = jnp.arange(GRID_W, dtype=F32)[:, None] * freqs

    def grid(tab_r, tab_c, sign):
        tr = jnp.broadcast_to(tab_r[:, None, :], (n_rows, GRID_W, f))
        tc = jnp.broadcast_to(tab_c[None, :, :], (n_rows, GRID_W, f))
        head = jnp.concatenate([sign * tr, tr, sign * tc, tc], axis=-1).reshape(seq, 4 * f)
        return jnp.concatenate([head, head], axis=1)

    cos = grid(jnp.cos(ang_r), jnp.cos(ang_c), 1.0)
    sin = grid(jnp.sin(ang_r), jnp.sin(ang_c), -1.0)
    cos = jnp.concatenate([jnp.ones((PAD_ROWS, cos.shape[1]), F32), cos], axis=0)
    sin = jnp.concatenate([jnp.zeros((PAD_ROWS, sin.shape[1]), F32), sin], axis=0)
    return cos, sin


def _in_projection(x, meta_tokens, w_in, norm_pre, q_norm, k_norm, conv_w, a_log, dt_bias):
    bsz, seq, _ = x.shape
    lp = PAD_ROWS + seq
    nt = lp // ROW_TILE
    per = ROW_TILE // HALO
    cw = jnp.pad(conv_w.astype(F32), ((0, 8 - CONV_W), (0, 0)))
    gate_pad = (2 * DN_HEADS, GATE_W - 4 * DN_HEADS)
    alog = jnp.pad(a_log.astype(F32).reshape(-1), gate_pad)[None]
    dtb = jnp.pad(dt_bias.astype(F32).reshape(-1), gate_pad)[None]
    ones_d = _block_diag_ones(DN_WIDTH, DN_HEAD_DIM)
    o = np.cumsum([0, DN_WIDTH * 3, DN_WIDTH, 2 * DN_HEADS, 2 * DN_HEADS, AT_WIDTH, AT_KV_WIDTH, AT_KV_WIDTH])
    wb = w_in.astype(BF16)
    w_qkv, w_z = wb[:, o[0]:o[1]], wb[:, o[1]:o[2]]
    w_ba = jnp.pad(wb[:, o[2]:o[4]], ((0, 0), (0, GATE_W - 4 * DN_HEADS)))
    w_q, w_k, w_v = wb[:, o[4]:o[5]], wb[:, o[5]:o[6]], wb[:, o[6]:o[7]]
    metap = jnp.pad(meta_tokens.astype(F32), ((PAD_ROWS - N_META, 0), (0, 0)))
    cos2, sin2 = _rope_tables(seq)
    qn = jnp.tile(q_norm.astype(F32), AT_Q_HEADS)[None]
    kn = jnp.tile(k_norm.astype(F32), AT_KV_HEADS)[None]

    def full(a):
        return pl.BlockSpec(a.shape, lambda b, i: (0,) * a.ndim)

    def rows(width):
        return pl.BlockSpec((1, ROW_TILE, width), lambda b, i: (b, i, 0))

    wpre = norm_pre.astype(F32)[None]
    ones_q = _block_diag_ones(AT_WIDTH, AT_HEAD_DIM)
    ones_k = _block_diag_ones(AT_KV_WIDTH, AT_HEAD_DIM)
    consts = (metap, wpre, w_qkv, w_z, w_ba, w_q, w_k, w_v, qn, kn, cw, alog, dtb)

    def cols(height):
        return pl.BlockSpec((1, height, ROW_TILE), lambda b, i: (b, 0, i))

    x_specs = [pl.BlockSpec((1, ROW_TILE, D_MODEL), lambda b, i: (b, jnp.maximum(i - 1, 0), 0)),
               pl.BlockSpec((1, HALO, D_MODEL), lambda b, i: (b, jnp.maximum((i - 1) * per - 1, 0), 0)),
               pl.BlockSpec((1, HALO, D_MODEL), lambda b, i: (b, jnp.minimum(i * per, seq // HALO - 1), 0))]
    q_spec = pl.BlockSpec((1, AT_WIDTH, ROW_TILE), lambda b, i: (b, 0, jnp.maximum(i - 1, 0)))
    out_specs = [rows(DN_WIDTH)] * 3 + [rows(GATE_W), rows(DN_WIDTH), q_spec, rows(2 * LANES), cols(2 * LANES)]
    out_shape = [jax.ShapeDtypeStruct((bsz, lp, DN_WIDTH), F32)] * 3 + [
        jax.ShapeDtypeStruct((bsz, lp, GATE_W), F32),
        jax.ShapeDtypeStruct((bsz, lp, DN_WIDTH), F32),
        jax.ShapeDtypeStruct((bsz, AT_WIDTH, seq), BF16),
        jax.ShapeDtypeStruct((bsz, lp, 2 * LANES), BF16),
        jax.ShapeDtypeStruct((bsz, 2 * LANES, lp), BF16)]
    return pl.pallas_call(
        _inproj_kernel,
        grid=(bsz, nt),
        in_specs=x_specs + [full(a) for a in consts]
                 + [pl.BlockSpec((ROW_TILE, LANES), lambda b, i: (i, 0))] * 2
                 + [full(ones_q), full(ones_k), full(ones_d)],
        out_specs=out_specs,
        out_shape=out_shape,
        compiler_params=_cparams(("parallel", "arbitrary")),
        name="in_projection",
    )(x, x, x, *consts, cos2, sin2, ones_q, ones_k, ones_d)


def _dot(a, b):
    return jnp.dot(a.astype(BF16), b.astype(BF16), preferred_element_type=F32)


def _dot_nt(a, b):
    return lax.dot_general(a.astype(BF16), b.astype(BF16), (((1,), (1,)), ((), ())),
                           preferred_element_type=F32)


INV_BLOCK = 32


def _unit_tri_inverses(neg_as, row, col):
    same = lambda w: (row // w) == (col // w)
    eye = jnp.where(row == col, 1.0, 0.0)
    xs = [jnp.where(same(INV_BLOCK), a, 0.0) for a in neg_as]
    ps = [eye + x for x in xs]
    for _ in range(int(math.log2(INV_BLOCK)) - 1):
        xs = [_dot(x, x) for x in xs]
        ps = [p + _dot(p, x) for p, x in zip(ps, xs)]
    w = INV_BLOCK
    while w < DN_CHUNK:
        band = jnp.logical_and(same(2 * w), jnp.logical_not(same(w)))
        ts = [_dot(p, jnp.where(band, a, 0.0)) for p, a in zip(ps, neg_as)]
        ps = [p + _dot(t, p) for p, t in zip(ps, ts)]
        w *= 2
    return [p - eye for p in ps]


def _delta_kernel(qf_ref, kf_ref, vf_ref, gf_ref, qb_ref, kb_ref, vb_ref, gb_ref,
                  of_ref, ob_ref, s_ref):
    s = pl.program_id(1)

    @pl.when(s == 0)
    def _():
        s_ref[...] = jnp.zeros_like(s_ref)

    c = DN_CHUNK
    hd = DN_HEAD_DIM
    nb = qf_ref.shape[0]
    row = lax.broadcasted_iota(jnp.int32, (c, c), 0)
    col = lax.broadcasted_iota(jnp.int32, (c, c), 1)
    dirs = ((qf_ref, kf_ref, vf_ref, gf_ref, of_ref), (qb_ref, kb_ref, vb_ref, gb_ref, ob_ref))
    scans = [(bi, d) for bi in range(nb) for d in range(2)]
    streams = [(e, h) for e in range(len(scans)) for h in range(DN_HEADS)]

    incl, strict, lmat = [], [], []
    for d in range(2):
        order = col - row if d == 0 else (col - row) * jnp.where(s == 0, 1, -1)
        incl.append(order <= 0)
        strict.append(order < 0)
        lmat.append(jnp.where(incl[d], 1.0, 0.0).astype(BF16))

    gates, gc_all, dm_all = [], [], []
    for bi, d in scans:
        g = dirs[d][3][bi]
        gates.append(g)
        g_hi, g_lo = _split_bf16(g)
        gc_all.append(jnp.dot(lmat[d], g_hi, preferred_element_type=F32)
                      + jnp.dot(lmat[d], g_lo, preferred_element_type=F32))
        parts = []
        for h in range(DN_HEADS):
            jg = 2 * DN_HEADS + d * DN_HEADS + h
            parts.extend(_split_bf16(jnp.where(strict[d], g[:, jg:jg + 1], 0.0)))
        dm = jnp.dot(lmat[d], jnp.concatenate(parts, axis=1), preferred_element_type=F32)
        dm_all.append([dm[:, (2 * h) * c:(2 * h + 1) * c] + dm[:, (2 * h + 1) * c:(2 * h + 2) * c]
                       for h in range(DN_HEADS)])

    beta, gc, gc_last, decay, egc, q, k, v = [], [], [], [], [], [], [], []
    for e, h in streams:
        bi, d = scans[e]
        jb = d * DN_HEADS + h
        jg = 2 * DN_HEADS + jb
        sl = slice(h * hd, (h + 1) * hd)
        beta.append(gates[e][:, jb:jb + 1])
        gcs = gc_all[e][:, jg:jg + 1]
        gc.append(gcs)
        gc_last.append(gcs[c - 1:c, :] if d == 0 else jnp.where(s == 0, gcs[c - 1:c, :], gcs[0:1, :]))
        decay.append(jnp.where(incl[d], jnp.exp(dm_all[e][h]), 0.0))
        egc.append(jnp.exp(gcs))
        q.append(dirs[d][0][bi, :, sl])
        k.append(dirs[d][1][bi, :, sl])
        v.append(dirs[d][2][bi, :, sl])

    n = len(streams)
    tri = [strict[scans[e][1]] for e, _ in streams]
    qkk = [_dot_nt(jnp.concatenate([q[i], k[i]], axis=0), k[i]) for i in range(n)]
    qk = [qkk[i][:c] * decay[i] for i in range(n)]
    neg_a = [jnp.where(tri[i], -(qkk[i][c:] * decay[i] * beta[i]), 0.0) for i in range(n)]
    tn = _unit_tri_inverses(neg_a, row, col)
    rhs = [jnp.concatenate([v[i] * beta[i], k[i] * (beta[i] * egc[i])], axis=1) for i in range(n)]
    uw = [rhs[i] + _dot(tn[i], rhs[i]) for i in range(n)]
    state = [s_ref[i] for i in range(n)]
    wq = [_dot(jnp.concatenate([uw[i][:, hd:], q[i] * egc[i]], axis=0), state[i]) for i in range(n)]
    v_new = [uw[i][:, :hd] - wq[i][:c] for i in range(n)]
    o = [wq[i][c:] + _dot(qk[i], v_new[i]) for i in range(n)]
    for i, (e, h) in enumerate(streams):
        bi, d = scans[e]
        dirs[d][4][bi, :, h * hd:(h + 1) * hd] = o[i]
    k_dec = [k[i] * jnp.exp(gc_last[i] - gc[i]) for i in range(n)]
    s_new = [state[i] * jnp.exp(gc_last[i]) + _dot(k_dec[i].T, v_new[i]) for i in range(n)]
    for i in range(n):
        s_ref[i] = s_new[i]


DN_BATCH = 2


def _delta_rule(q, k, v, gates, seq):
    bsz, lp, _ = q.shape
    ns = 1 + seq // DN_CHUNK
    nb = math.gcd(DN_BATCH, bsz)

    def fwd(b, s):
        return (b, META_CHUNK + s, 0)

    def bwd(b, s):
        return (b, jnp.where(s == 0, META_CHUNK, META_CHUNK + ns - s), 0)

    def spec(w, imap):
        return pl.BlockSpec((nb, DN_CHUNK, w), imap)

    ins = [spec(DN_WIDTH, fwd)] * 3 + [spec(GATE_W, fwd)] + [spec(DN_WIDTH, bwd)] * 3 + [spec(GATE_W, bwd)]
    return pl.pallas_call(
        _delta_kernel,
        grid=(bsz // nb, ns),
        in_specs=ins,
        out_specs=[spec(DN_WIDTH, fwd), spec(DN_WIDTH, bwd)],
        out_shape=[jax.ShapeDtypeStruct((bsz, lp, DN_WIDTH), F32)] * 2,
        scratch_shapes=[pltpu.VMEM((nb * 2 * DN_HEADS, DN_HEAD_DIM, DN_HEAD_DIM), F32)],
        compiler_params=_cparams(("parallel", "arbitrary")),
        name="delta_rule",
    )(q, k, v, gates, q, k, v, gates)


MXU_DIM = 256
KEY_START = PAD_ROWS - MXU_DIM
GROUP = AT_Q_HEADS // AT_KV_HEADS
Q_TILE = 512


def _kv_tile(n_keys):
    units = n_keys // MXU_DIM
    best = max(u for u in range(1, 4) if units % u == 0)
    return best * MXU_DIM


SUBLANES = 8


def _attn_kernel(qt_ref, k_ref, vt_ref, o_ref, qp_ref, m_ref, acc_ref, s_ref, *, tk, nkb):
    tq = qt_ref.shape[2]
    hd = AT_HEAD_DIM
    for j in range(GROUP):
        qp_ref[j, :hd, :] = qt_ref[0, j * hd:(j + 1) * hd, :]
        qp_ref[j, hd:, :] = jnp.zeros((LANES - hd, tq), BF16)

    def start_of(kb):
        start = KEY_START + kb * tk
        return start if isinstance(kb, int) else pl.multiple_of(start, LANES)

    def scores(j, kb):
        return jnp.dot(k_ref[0, pl.ds(start_of(kb), tk), :], qp_ref[j], preferred_element_type=F32)

    def col_max(s):
        m = s[:SUBLANES]
        for t in range(1, tk // SUBLANES):
            m = jnp.maximum(m, s[t * SUBLANES:(t + 1) * SUBLANES])
        return jnp.broadcast_to(jnp.max(m, axis=0, keepdims=True), (SUBLANES, tq))

    def rows(m, n):
        return jnp.concatenate([m] * (n // SUBLANES), axis=0)

    def unit(kb, j, first):
        if j + 1 < GROUP:
            nxt_kb = kb
        elif isinstance(kb, int):
            nxt_kb = min(kb + 1, nkb - 1)
        else:
            nxt_kb = jnp.minimum(kb + 1, nkb - 1)
        s_ref[(j + 1) % 2] = scores((j + 1) % GROUP, nxt_kb)
        s = s_ref[j % 2]
        vt = vt_ref[0, :, pl.ds(start_of(kb), tk)]
        if first:
            is_key = lax.broadcasted_iota(jnp.int32, (tk, tq), 0) >= PAD_ROWS - N_META - KEY_START
            s = jnp.where(is_key, s, -jnp.inf)
            m_new = col_max(s)
            p = jnp.exp2(s - rows(m_new, tk))
            acc_ref[j] = jnp.dot(vt, p.astype(BF16), preferred_element_type=F32)
        else:
            m_prev = m_ref[j]
            m_new = jnp.maximum(m_prev, col_max(s))
            alpha = jnp.exp2(m_prev - m_new)
            p = jnp.exp2(s - rows(m_new, tk))
            acc_ref[j] = rows(alpha, LANES) * acc_ref[j] + jnp.dot(vt, p.astype(BF16), preferred_element_type=F32)
        m_ref[j] = m_new

    s_ref[0] = scores(0, 0)
    for j in range(GROUP):
        unit(0, j, True)

    def body(kb, carry):
        for j in range(GROUP):
            unit(kb, j, False)
        return carry

    trips = nkb - 1
    lax.fori_loop(1, nkb, body, 0, unroll=max(u for u in (1, 2, 3, 4, 5) if trips % u == 0))

    outs = []
    for j in range(GROUP):
        acc = acc_ref[j]
        outs.append(acc[:hd] * (1.0 / acc[hd:hd + 1]))
    o_ref[0] = jnp.concatenate(outs, axis=0).T.astype(o_ref.dtype)


def _attention(aqt, ak, avt, seq):
    bsz, lp, _ = ak.shape
    tq = math.gcd(Q_TILE, seq)
    n_keys = lp - KEY_START
    tk = _kv_tile(n_keys)
    gw = GROUP * AT_HEAD_DIM
    assert GROUP % 2 == 0
    return pl.pallas_call(
        functools.partial(_attn_kernel, tk=tk, nkb=n_keys // tk),
        grid=(bsz, AT_KV_HEADS, seq // tq),
        in_specs=[pl.BlockSpec((1, gw, tq), lambda b, g, i: (b, g, i)),
                  pl.BlockSpec((1, lp, LANES), lambda b, g, i: (b, 0, g)),
                  pl.BlockSpec((1, LANES, lp), lambda b, g, i: (b, g, 0))],
        out_specs=pl.BlockSpec((1, tq, gw), lambda b, g, i: (b, i, g)),
        out_shape=jax.ShapeDtypeStruct((bsz, seq, AT_WIDTH), BF16),
        scratch_shapes=[pltpu.VMEM((GROUP, LANES, tq), BF16),
                        pltpu.VMEM((GROUP, SUBLANES, tq), F32),
                        pltpu.VMEM((GROUP, LANES, tq), F32),
                        pltpu.VMEM((2, tk, tq), F32)],
        compiler_params=_cparams(("parallel", "parallel", "parallel")),
        name="gqa_attention",
    )(aqt, ak, avt)


FF_TILE = 1024


def _rms(x, w):
    return x * lax.rsqrt(jnp.mean(x * x, axis=-1, keepdims=True) + EPS) * w


def _out_mlp_kernel(x_ref, of_ref, ob_ref, z_ref, at_ref, dnw_ref, ones_ref, wod_ref, woa_ref, npost_ref,
                    nmpre_ref, wup_ref, wdn_ref, nmpost_ref, out_ref):
    rows = ROW_TILE // ROW_SPLIT
    slabs = [slice(p * rows, (p + 1) * rows) for p in range(ROW_SPLIT)]

    dn = []
    for sl in slabs:
        o = of_ref[0, sl, :] + ob_ref[0, sl, :]
        o = o * lax.rsqrt(_group_mean(o * o, ones_ref[...]) + EPS) * dnw_ref[...]
        z = z_ref[0, sl, :]
        dn.append((o * (z * _sigmoid(z))).astype(BF16))
    mix = [jnp.dot(dn[p], wod_ref[...], preferred_element_type=F32)
           + jnp.dot(at_ref[0, sl, :], woa_ref[...], preferred_element_type=F32) for p, sl in enumerate(slabs)]
    h1 = [x_ref[0, sl, :] + _rms(mix[p], npost_ref[...]) for p, sl in enumerate(slabs)]
    u = [_rms(h, nmpre_ref[...]).astype(BF16) for h in h1]

    def up(p, j):
        return jnp.dot(u[p], wup_ref[:, j * FF_TILE:(j + 1) * FF_TILE], preferred_element_type=F32)

    n_ff = D_FF // FF_TILE
    f = [None] * ROW_SPLIT
    a = [up(p, 0) for p in range(ROW_SPLIT)]
    for j in range(n_ff):
        a_next = [up(p, j + 1) for p in range(ROW_SPLIT)] if j + 1 < n_ff else None
        for p in range(ROW_SPLIT):
            act = jnp.maximum(a[p], 0.0)
            part = jnp.dot((act * act).astype(BF16), wdn_ref[j * FF_TILE:(j + 1) * FF_TILE, :],
                           preferred_element_type=F32)
            f[p] = part if f[p] is None else f[p] + part
        a = a_next
    for p, sl in enumerate(slabs):
        out_ref[0, sl, :] = h1[p] + _rms(f[p], nmpost_ref[...])


def _out_mlp(x, o_f, o_b, z, at, dn_out_norm, w_out, norm_mix_post, norm_mlp_pre, w_up, w_down, norm_mlp_post):
    bsz, seq, _ = x.shape
    off = PAD_ROWS // ROW_TILE
    dnw = jnp.tile(dn_out_norm.astype(F32), DN_HEADS)[None]
    ones = _block_diag_ones(DN_WIDTH, DN_HEAD_DIM)
    wo = w_out.astype(BF16)
    consts = (dnw, ones, wo[:DN_WIDTH], wo[DN_WIDTH:], norm_mix_post.astype(F32)[None],
              norm_mlp_pre.astype(F32)[None], w_up.astype(BF16), w_down.astype(BF16),
              norm_mlp_post.astype(F32)[None])

    def full(a):
        return pl.BlockSpec(a.shape, lambda b, i: (0,) * a.ndim, pipeline_mode=pl.Buffered(1))

    def real(w):
        return pl.BlockSpec((1, ROW_TILE, w), lambda b, i: (b, i, 0))

    def padded(w):
        return pl.BlockSpec((1, ROW_TILE, w), lambda b, i: (b, i + off, 0))

    return pl.pallas_call(
        _out_mlp_kernel,
        grid=(bsz, seq // ROW_TILE),
        in_specs=[real(D_MODEL), padded(DN_WIDTH), padded(DN_WIDTH), padded(DN_WIDTH), real(AT_WIDTH)]
                 + [full(a) for a in consts],
        out_specs=real(D_MODEL),
        out_shape=jax.ShapeDtypeStruct((bsz, seq, D_MODEL), x.dtype),
        compiler_params=_cparams(("parallel", "parallel")),
        name="out_mlp",
    )(x, o_f, o_b, z, at, *consts)


def kernel(x, meta_tokens, w_in, conv_w, a_log, dt_bias, dn_out_norm, q_norm, k_norm, w_out,
           norm_mix_pre, norm_mix_post, w_up, w_down, norm_mlp_pre, norm_mlp_post):
    assert w_in.shape[0] == 1, "single-layer block: meta-token outputs are not materialised"
    bsz, seq, d = x.shape
    assert d == D_MODEL and seq % ROW_TILE == 0 and seq % GRID_W == 0
    dq, dk, dv, gates, z, aq, ak, av = _in_projection(x, meta_tokens, w_in[0], norm_mix_pre[0], q_norm[0],
                                                      k_norm[0], conv_w[0], a_log[0], dt_bias[0])
    o_f, o_b = _delta_rule(dq, dk, dv, gates, seq)
    at = _attention(aq, ak, av, seq)
    return _out_mlp(x, o_f, o_b, z, at, dn_out_norm[0], w_out[0], norm_mix_post[0], norm_mlp_pre[0],
                    w_up[0], w_down[0], norm_mlp_post[0])
```

```python
import functools
import math

import numpy as np
import jax
import jax.numpy as jnp
from jax import lax
from jax.experimental import pallas as pl
from jax.experimental.pallas import tpu as pltpu

F32 = jnp.float32
BF16 = jnp.bfloat16

D_MODEL = 1024
N_META = 16
GRID_W = 64
DN_HEADS = 4
DN_HEAD_DIM = 128
DN_WIDTH = DN_HEADS * DN_HEAD_DIM
CONV_W = 5
AT_Q_HEADS = 8
AT_KV_HEADS = 2
AT_HEAD_DIM = 64
AT_WIDTH = AT_Q_HEADS * AT_HEAD_DIM
AT_KV_WIDTH = AT_KV_HEADS * AT_HEAD_DIM
ROPE_THETA = 10000.0
ROPE_AXIS_DIM = AT_HEAD_DIM // 2
D_FF = 4 * D_MODEL
EPS = 1e-6

LANES = 128
SUBLANES = 8
MXU_DIM = 256
HALO = 8
ROW_TILE = 512
ROW_SPLIT = 2
IN_SPLIT = 2
PAD_ROWS = ROW_TILE
DN_CHUNK = 128
META_CHUNK = PAD_ROWS // DN_CHUNK - 1
GATE_W = LANES
VMEM_LIMIT = 56 * 1024 * 1024


def _cparams(sem):
    return pltpu.CompilerParams(dimension_semantics=sem, vmem_limit_bytes=VMEM_LIMIT)


def _split_bf16(x):
    hi = x.astype(BF16)
    lo = (x - hi.astype(F32)).astype(BF16)
    return hi, lo


def _group_mean(sq, ones_bd):
    return jnp.dot(sq.astype(BF16), ones_bd, preferred_element_type=F32)


def _block_diag_ones(width, group):
    idx = np.arange(width) // group
    return jnp.asarray((idx[:, None] == idx[None, :]).astype(np.float32) / group, dtype=BF16)


def _sigmoid(x):
    return 1.0 / (1.0 + jnp.exp(-x))


def _rope(x, cos, sin_signed, width):
    half = ROPE_AXIS_DIM // 2
    lane = lax.broadcasted_iota(jnp.int32, x.shape, 1)
    partner = jnp.where((lane % ROPE_AXIS_DIM) < half,
                        pltpu.roll(x, width - half, 1),
                        pltpu.roll(x, half, 1))
    return x * cos + partner * sin_signed


def _softplus(x):
    return jnp.maximum(x, 0.0) + jnp.log1p(jnp.exp(-jnp.abs(x)))


def _inproj_kernel(x_ref, xprev_ref, xnext_ref, metap_ref, wpre_ref, wqkv_ref, wz_ref, wba_ref, wq_ref, wk_ref,
                   wv_ref, qn_ref, kn_ref, cw_ref, alog_ref, dtb_ref, cos_ref, sin_ref, onesq_ref, onesk_ref,
                   onesd_ref, dq_ref, dk_ref, dv_ref, g_ref, z_ref, aq_ref, ak_ref, av_ref):
    i = pl.program_id(1)
    last = pl.num_programs(1) - 1
    rows = ROW_TILE // IN_SPLIT
    ext = rows + 2 * HALO
    slabs = [slice(p * rows, (p + 1) * rows) for p in range(IN_SPLIT)]

    hc = jnp.where(i == 0, metap_ref[...], x_ref[0])
    hp = jnp.where(i == 0, 0.0, jnp.where(i == 1, metap_ref[ROW_TILE - HALO:, :], xprev_ref[0]))
    hn = jnp.where(i == last, 0.0, xnext_ref[0])
    he = jnp.concatenate([hp, hc, hn], axis=0)

    def project(p):
        hs = he[p * rows:p * rows + ext]
        ms = jnp.mean(hs * hs, axis=-1, keepdims=True)
        ue = hs * lax.rsqrt(ms + EPS) * wpre_ref[...]
        uc = ue[HALO:HALO + rows].astype(BF16)
        z_ref[0, slabs[p], :] = jnp.dot(uc, wz_ref[...], preferred_element_type=F32)
        return (jnp.dot(ue.astype(BF16), wqkv_ref[...], preferred_element_type=F32),
                jnp.dot(uc, wba_ref[...], preferred_element_type=F32),
                jnp.dot(uc, wq_ref[...], preferred_element_type=F32),
                jnp.dot(uc, wk_ref[...], preferred_element_type=F32),
                jnp.dot(uc, wv_ref[...], preferred_element_type=F32))

    reps = AT_WIDTH // LANES
    tail = jnp.where(lax.broadcasted_iota(jnp.int32, (rows, AT_HEAD_DIM), 1) == 0, 1.0, 0.0)
    nxt = project(0)
    for p, sl in enumerate(slabs):
        qkv, raw, q, k, v = nxt
        if p + 1 < IN_SPLIT:
            nxt = project(p + 1)

        acc = None
        for j in range(CONV_W):
            shift = (CONV_W // 2 - j) % ext
            shifted = qkv if shift == 0 else pltpu.roll(qkv, shift, 0)
            term = shifted[HALO:HALO + rows] * cw_ref[j:j + 1, :]
            acc = term if acc is None else acc + term
        y = acc * _sigmoid(acc)

        ones = onesd_ref[...]
        dq = y[:, :DN_WIDTH]
        dk = y[:, DN_WIDTH:2 * DN_WIDTH]
        dq = dq * lax.rsqrt(_group_mean(dq * dq, ones) * DN_HEAD_DIM + EPS) * (DN_HEAD_DIM ** -0.5)
        dk = dk * lax.rsqrt(_group_mean(dk * dk, ones) * DN_HEAD_DIM + EPS)
        dq_ref[0, sl, :] = dq
        dk_ref[0, sl, :] = dk
        dv_ref[0, sl, :] = y[:, 2 * DN_WIDTH:]

        row = lax.broadcasted_iota(jnp.int32, (rows, 1), 0) + p * rows
        valid = jnp.logical_or(i > 0, row >= PAD_ROWS - N_META)

        lane = lax.broadcasted_iota(jnp.int32, raw.shape, 1)
        g = -jnp.exp(alog_ref[...]) * _softplus(raw + dtb_ref[...])
        gates = jnp.where(lane < 2 * DN_HEADS, _sigmoid(raw), jnp.where(lane < 4 * DN_HEADS, g, 0.0))
        g_ref[0, sl, :] = jnp.where(valid, gates, 0.0)

        cos2 = cos_ref[sl, :]
        sin2 = sin_ref[sl, :]
        qs = q * lax.rsqrt(_group_mean(q * q, onesq_ref[...]) + EPS) * qn_ref[...]
        qs = _rope(qs, jnp.concatenate([cos2] * reps, axis=1), jnp.concatenate([sin2] * reps, axis=1), AT_WIDTH)
        aq_ref[0, :, sl] = (qs * (AT_HEAD_DIM ** -0.5 * math.log2(math.e))).T.astype(BF16)

        ks = k * lax.rsqrt(_group_mean(k * k, onesk_ref[...]) + EPS) * kn_ref[...]
        ks = _rope(ks, cos2, sin2, AT_KV_WIDTH).astype(BF16)
        ak_ref[0, sl, :] = jnp.concatenate([ks[:, :AT_HEAD_DIM], ks[:, :AT_HEAD_DIM],
                                            ks[:, AT_HEAD_DIM:], ks[:, AT_HEAD_DIM:]], axis=1)

        av_ref[0, :, sl] = jnp.concatenate([v[:, :AT_HEAD_DIM], tail, v[:, AT_HEAD_DIM:], tail],
                                           axis=1).T.astype(BF16)


def _rope_tables(seq):
    f = ROPE_AXIS_DIM // 2
    n_rows = seq // GRID_W
    freqs = ROPE_THETA ** (-jnp.arange(f, dtype=F32) / f)
    ang_r = jnp.arange(n_rows, dtype=F32)[:, None] * freqs
    ang_c = jnp.arange(GRID_W, dtype=F32)[:, None] * freqs

    def grid(tab_r, tab_c, sign):
        tr = jnp.broadcast_to(tab_r[:, None, :], (n_rows, GRID_W, f))
        tc = jnp.broadcast_to(tab_c[None, :, :], (n_rows, GRID_W, f))
        head = jnp.concatenate([sign * tr, tr, sign * tc, tc], axis=-1).reshape(seq, 4 * f)
        return jnp.concatenate([head, head], axis=1)

    cos = grid(jnp.cos(ang_r), jnp.cos(ang_c), 1.0)
    sin = grid(jnp.sin(ang_r), jnp.sin(ang_c), -1.0)
    cos = jnp.concatenate([jnp.ones((PAD_ROWS, cos.shape[1]), F32), cos], axis=0)
    sin = jnp.concatenate([jnp.zeros((PAD_ROWS, sin.shape[1]), F32), sin], axis=0)
    return cos, sin


def _in_projection(x, meta_tokens, w_in, norm_pre, q_norm, k_norm, conv_w, a_log, dt_bias):
    bsz, seq, _ = x.shape
    lp = PAD_ROWS + seq
    nt = lp // ROW_TILE
    per = ROW_TILE // HALO
    cw = jnp.pad(conv_w.astype(F32), ((0, 8 - CONV_W), (0, 0)))
    gate_pad = (2 * DN_HEADS, GATE_W - 4 * DN_HEADS)
    alog = jnp.pad(a_log.astype(F32).reshape(-1), gate_pad)[None]
    dtb = jnp.pad(dt_bias.astype(F32).reshape(-1), gate_pad)[None]
    ones_d = _block_diag_ones(DN_WIDTH, DN_HEAD_DIM)
    o = np.cumsum([0, DN_WIDTH * 3, DN_WIDTH, 2 * DN_HEADS, 2 * DN_HEADS, AT_WIDTH, AT_KV_WIDTH, AT_KV_WIDTH])
    wb = w_in.astype(BF16)
    w_qkv, w_z = wb[:, o[0]:o[1]], wb[:, o[1]:o[2]]
    w_ba = jnp.pad(wb[:, o[2]:o[4]], ((0, 0), (0, GATE_W - 4 * DN_HEADS)))
    w_q, w_k, w_v = wb[:, o[4]:o[5]], wb[:, o[5]:o[6]], wb[:, o[6]:o[7]]
    metap = jnp.pad(meta_tokens.astype(F32), ((PAD_ROWS - N_META, 0), (0, 0)))
    cos2, sin2 = _rope_tables(seq)
    qn = jnp.tile(q_norm.astype(F32), AT_Q_HEADS)[None]
    kn = jnp.tile(k_norm.astype(F32), AT_KV_HEADS)[None]

    def full(a):
        return pl.BlockSpec(a.shape, lambda b, i: (0,) * a.ndim)

    def rows(width):
        return pl.BlockSpec((1, ROW_TILE, width), lambda b, i: (b, i, 0))

    wpre = norm_pre.astype(F32)[None]
    ones_q = _block_diag_ones(AT_WIDTH, AT_HEAD_DIM)
    ones_k = _block_diag_ones(AT_KV_WIDTH, AT_HEAD_DIM)
    consts = (metap, wpre, w_qkv, w_z, w_ba, w_q, w_k, w_v, qn, kn, cw, alog, dtb)

    def cols(height):
        return pl.BlockSpec((1, height, ROW_TILE), lambda b, i: (b, 0, i))

    x_specs = [pl.BlockSpec((1, ROW_TILE, D_MODEL), lambda b, i: (b, jnp.maximum(i - 1, 0), 0)),
               pl.BlockSpec((1, HALO, D_MODEL), lambda b, i: (b, jnp.maximum((i - 1) * per - 1, 0), 0)),
               pl.BlockSpec((1, HALO, D_MODEL), lambda b, i: (b, jnp.minimum(i * per, seq // HALO - 1), 0))]
    q_spec = pl.BlockSpec((1, AT_WIDTH, ROW_TILE), lambda b, i: (b, 0, jnp.maximum(i - 1, 0)))
    out_specs = [rows(DN_WIDTH)] * 3 + [rows(GATE_W), rows(DN_WIDTH), q_spec, rows(2 * LANES), cols(2 * LANES)]
    out_shape = [jax.ShapeDtypeStruct((bsz, lp, DN_WIDTH), F32)] * 3 + [
        jax.ShapeDtypeStruct((bsz, lp, GATE_W), F32),
        jax.ShapeDtypeStruct((bsz, lp, DN_WIDTH), F32),
        jax.ShapeDtypeStruct((bsz, AT_WIDTH, seq), BF16),
        jax.ShapeDtypeStruct((bsz, lp, 2 * LANES), BF16),
        jax.ShapeDtypeStruct((bsz, 2 * LANES, lp), BF16)]
    return pl.pallas_call(
        _inproj_kernel,
        grid=(bsz, nt),
        in_specs=x_specs + [full(a) for a in consts]
                 + [pl.BlockSpec((ROW_TILE, LANES), lambda b, i: (i, 0))] * 2
                 + [full(ones_q), full(ones_k), full(ones_d)],
        out_specs=out_specs,
        out_shape=out_shape,
        compiler_params=_cparams(("parallel", "arbitrary")),
        name="in_projection",
    )(x, x, x, *consts, cos2, sin2, ones_q, ones_k, ones_d)


def _dot(a, b):
    return jnp.dot(a.astype(BF16), b.astype(BF16), preferred_element_type=F32)


def _dot_nt(a, b):
    return lax.dot_general(a.astype(BF16), b.astype(BF16), (((1,), (1,)), ((), ())),
                           preferred_element_type=F32)


INV_BLOCK = 8


def _unit_tri_inverses(neg_as, row, col):
    same = lambda w: (row // w) == (col // w)
    eye = jnp.where(row == col, 1.0, 0.0)
    xs = [jnp.where(same(INV_BLOCK), a, 0.0) for a in neg_as]
    ps = [eye + x for x in xs]
    for _ in range(int(math.log2(INV_BLOCK)) - 1):
        xs = [_dot(x, x) for x in xs]
        ps = [p + _dot(p, x) for p, x in zip(ps, xs)]
    w = INV_BLOCK
    while w < DN_CHUNK:
        band = jnp.logical_and(same(2 * w), jnp.logical_not(same(w)))
        ts = [_dot(p, jnp.where(band, a, 0.0)) for p, a in zip(ps, neg_as)]
        ps = [p + _dot(t, p) for p, t in zip(ps, ts)]
        w *= 2
    return [p - eye for p in ps]


def _delta_kernel(qf_ref, kf_ref, vf_ref, gf_ref, qb_ref, kb_ref, vb_ref, gb_ref,
                  of_ref, ob_ref, s_ref):
    s = pl.program_id(1)

    @pl.when(s == 0)
    def _():
        s_ref[...] = jnp.zeros_like(s_ref)

    c = DN_CHUNK
    hd = DN_HEAD_DIM
    nb = qf_ref.shape[0]
    row = lax.broadcasted_iota(jnp.int32, (c, c), 0)
    col = lax.broadcasted_iota(jnp.int32, (c, c), 1)
    dirs = ((qf_ref, kf_ref, vf_ref, gf_ref, of_ref), (qb_ref, kb_ref, vb_ref, gb_ref, ob_ref))
    scans = [(bi, d) for bi in range(nb) for d in range(2)]
    streams = [(e, h) for e in range(len(scans)) for h in range(DN_HEADS)]

    incl, strict, lmat = [], [], []
    for d in range(2):
        order = col - row if d == 0 else (col - row) * jnp.where(s == 0, 1, -1)
        incl.append(order <= 0)
        strict.append(order < 0)
        lmat.append(jnp.where(incl[d], 1.0, 0.0).astype(BF16))

    gates, gc_all, dm_all = [], [], []
    for bi, d in scans:
        g = dirs[d][3][bi]
        gates.append(g)
        g_hi, g_lo = _split_bf16(g)
        gc_all.append(jnp.dot(lmat[d], g_hi, preferred_element_type=F32)
                      + jnp.dot(lmat[d], g_lo, preferred_element_type=F32))
        parts = []
        for h in range(DN_HEADS):
            jg = 2 * DN_HEADS + d * DN_HEADS + h
            parts.extend(_split_bf16(jnp.where(strict[d], g[:, jg:jg + 1], 0.0)))
        dm = jnp.dot(lmat[d], jnp.concatenate(parts, axis=1), preferred_element_type=F32)
        dm_all.append([dm[:, (2 * h) * c:(2 * h + 1) * c] + dm[:, (2 * h + 1) * c:(2 * h + 2) * c]
                       for h in range(DN_HEADS)])

    beta, gc, gc_last, decay, egc, q, k, v = [], [], [], [], [], [], [], []
    for e, h in streams:
        bi, d = scans[e]
        jb = d * DN_HEADS + h
        jg = 2 * DN_HEADS + jb
        sl = slice(h * hd, (h + 1) * hd)
        beta.append(gates[e][:, jb:jb + 1])
        gcs = gc_all[e][:, jg:jg + 1]
        gc.append(gcs)
        gc_last.append(gcs[c - 1:c, :] if d == 0 else jnp.where(s == 0, gcs[c - 1:c, :], gcs[0:1, :]))
        decay.append(jnp.where(incl[d], jnp.exp(dm_all[e][h]), 0.0))
        egc.append(jnp.exp(gcs))
        q.append(dirs[d][0][bi, :, sl])
        k.append(dirs[d][1][bi, :, sl])
        v.append(dirs[d][2][bi, :, sl])

    n = len(streams)
    tri = [strict[scans[e][1]] for e, _ in streams]
    qkk = [_dot_nt(jnp.concatenate([q[i], k[i]], axis=0), k[i]) for i in range(n)]
    qk = [qkk[i][:c] * decay[i] for i in range(n)]
    neg_a = [jnp.where(tri[i], -(qkk[i][c:] * decay[i] * beta[i]), 0.0) for i in range(n)]
    tn = _unit_tri_inverses(neg_a, row, col)
    rhs = [jnp.concatenate([v[i] * beta[i], k[i] * (beta[i] * egc[i])], axis=1) for i in range(n)]
    uw = [rhs[i] + _dot(tn[i], rhs[i]) for i in range(n)]
    state = [s_ref[i] for i in range(n)]
    wq = [_dot(jnp.concatenate([uw[i][:, hd:], q[i] * egc[i]], axis=0), state[i]) for i in range(n)]
    v_new = [uw[i][:, :hd] - wq[i][:c] for i in range(n)]
    o = [wq[i][c:] + _dot(qk[i], v_new[i]) for i in range(n)]
    for i, (e, h) in enumerate(streams):
        bi, d = scans[e]
        dirs[d][4][bi, :, h * hd:(h + 1) * hd] = o[i]
    k_dec = [k[i] * jnp.exp(gc_last[i] - gc[i]) for i in range(n)]
    s_new = [state[i] * jnp.exp(gc_last[i]) + _dot(k_dec[i].T, v_new[i]) for i in range(n)]
    for i in range(n):
        s_ref[i] = s_new[i]


DN_BATCH = 2


def _delta_rule(q, k, v, gates, seq):
    bsz, lp, _ = q.shape
    ns = 1 + seq // DN_CHUNK
    nb = math.gcd(DN_BATCH, bsz)

    def fwd(b, s):
        return (b, META_CHUNK + s, 0)

    def bwd(b, s):
        return (b, jnp.where(s == 0, META_CHUNK, META_CHUNK + ns - s), 0)

    def spec(w, imap):
        return pl.BlockSpec((nb, DN_CHUNK, w), imap)

    ins = [spec(DN_WIDTH, fwd)] * 3 + [spec(GATE_W, fwd)] + [spec(DN_WIDTH, bwd)] * 3 + [spec(GATE_W, bwd)]
    return pl.pallas_call(
        _delta_kernel,
        grid=(bsz // nb, ns),
        in_specs=ins,
        out_specs=[spec(DN_WIDTH, fwd), spec(DN_WIDTH, bwd)],
        out_shape=[jax.ShapeDtypeStruct((bsz, lp, DN_WIDTH), F32)] * 2,
        scratch_shapes=[pltpu.VMEM((nb * 2 * DN_HEADS, DN_HEAD_DIM, DN_HEAD_DIM), F32)],
        compiler_params=_cparams(("parallel", "arbitrary")),
        name="delta_rule",
    )(q, k, v, gates, q, k, v, gates)


KEY_START = PAD_ROWS - MXU_DIM
GROUP = AT_Q_HEADS // AT_KV_HEADS
Q_TILE = 512


def _kv_tile(n_keys):
    units = n_keys // MXU_DIM
    best = max(u for u in range(1, 4) if units % u == 0)
    return best * MXU_DIM


def _attn_kernel(qt_ref, k_ref, vt_ref, o_ref, qp_ref, m_ref, acc_ref, s_ref, *, tk, nkb):
    tq = qt_ref.shape[2]
    hd = AT_HEAD_DIM
    for j in range(GROUP):
        qp_ref[j, :hd, :] = qt_ref[0, j * hd:(j + 1) * hd, :]
        qp_ref[j, hd:, :] = jnp.zeros((LANES - hd, tq), BF16)

    def start_of(kb):
        start = KEY_START + kb * tk
        return start if isinstance(kb, int) else pl.multiple_of(start, LANES)

    def scores(j, kb):
        return jnp.dot(k_ref[0, pl.ds(start_of(kb), tk), :], qp_ref[j], preferred_element_type=F32)

    def col_max(s):
        m = s[:SUBLANES]
        for t in range(1, tk // SUBLANES):
            m = jnp.maximum(m, s[t * SUBLANES:(t + 1) * SUBLANES])
        return jnp.broadcast_to(jnp.max(m, axis=0, keepdims=True), (SUBLANES, tq))

    def rows(m, n):
        return jnp.concatenate([m] * (n // SUBLANES), axis=0)

    def unit(kb, j, first):
        if j + 1 < GROUP:
            nxt_kb = kb
        elif isinstance(kb, int):
            nxt_kb = min(kb + 1, nkb - 1)
        else:
            nxt_kb = jnp.minimum(kb + 1, nkb - 1)
        s_ref[(j + 1) % 2] = scores((j + 1) % GROUP, nxt_kb)
        s = s_ref[j % 2]
        vt = vt_ref[0, :, pl.ds(start_of(kb), tk)]
        if first:
            is_key = lax.broadcasted_iota(jnp.int32, (tk, tq), 0) >= PAD_ROWS - N_META - KEY_START
            s = jnp.where(is_key, s, -jnp.inf)
            m_new = col_max(s)
            p = jnp.exp2(s - rows(m_new, tk))
            acc_ref[j] = jnp.dot(vt, p.astype(BF16), preferred_element_type=F32)
        else:
            m_prev = m_ref[j]
            m_new = jnp.maximum(m_prev, col_max(s))
            alpha = jnp.exp2(m_prev - m_new)
            p = jnp.exp2(s - rows(m_new, tk))
            acc_ref[j] = rows(alpha, LANES) * acc_ref[j] + jnp.dot(vt, p.astype(BF16), preferred_element_type=F32)
        m_ref[j] = m_new

    s_ref[0] = scores(0, 0)
    for j in range(GROUP):
        unit(0, j, True)

    def body(kb, carry):
        for j in range(GROUP):
            unit(kb, j, False)
        return carry

    trips = nkb - 1
    lax.fori_loop(1, nkb, body, 0, unroll=max(u for u in (1, 2, 3, 4, 5) if trips % u == 0))

    outs = []
    for j in range(GROUP):
        acc = acc_ref[j]
        outs.append(acc[:hd] * (1.0 / acc[hd:hd + 1]))
    o_ref[0] = jnp.concatenate(outs, axis=0).T.astype(o_ref.dtype)


def _attention(aqt, ak, avt, seq):
    bsz, lp, _ = ak.shape
    tq = math.gcd(Q_TILE, seq)
    n_keys = lp - KEY_START
    tk = _kv_tile(n_keys)
    gw = GROUP * AT_HEAD_DIM
    assert GROUP % 2 == 0
    return pl.pallas_call(
        functools.partial(_attn_kernel, tk=tk, nkb=n_keys // tk),
        grid=(bsz, AT_KV_HEADS, seq // tq),
        in_specs=[pl.BlockSpec((1, gw, tq), lambda b, g, i: (b, g, i)),
                  pl.BlockSpec((1, lp, LANES), lambda b, g, i: (b, 0, g)),
                  pl.BlockSpec((1, LANES, lp), lambda b, g, i: (b, g, 0))],
        out_specs=pl.BlockSpec((1, tq, gw), lambda b, g, i: (b, i, g)),
        out_shape=jax.ShapeDtypeStruct((bsz, seq, AT_WIDTH), BF16),
        scratch_shapes=[pltpu.VMEM((GROUP, LANES, tq), BF16),
                        pltpu.VMEM((GROUP, SUBLANES, tq), F32),
                        pltpu.VMEM((GROUP, LANES, tq), F32),
                        pltpu.VMEM((2, tk, tq), F32)],
        compiler_params=_cparams(("parallel", "parallel", "parallel")),
        name="gqa_attention",
    )(aqt, ak, avt)


FF_TILE = 1024


def _rms(x, w):
    return x * lax.rsqrt(jnp.mean(x * x, axis=-1, keepdims=True) + EPS) * w


def _out_mlp_kernel(x_ref, of_ref, ob_ref, z_ref, at_ref, dnw_ref, ones_ref, wod_ref, woa_ref, npost_ref,
                    nmpre_ref, wup_ref, wdn_ref, nmpost_ref, out_ref):
    rows = ROW_TILE // ROW_SPLIT
    slabs = [slice(p * rows, (p + 1) * rows) for p in range(ROW_SPLIT)]

    dn = []
    for sl in slabs:
        o = of_ref[0, sl, :] + ob_ref[0, sl, :]
        o = o * lax.rsqrt(_group_mean(o * o, ones_ref[...]) + EPS) * dnw_ref[...]
        z = z_ref[0, sl, :]
        dn.append((o * (z * _sigmoid(z))).astype(BF16))
    mix = [jnp.dot(dn[p], wod_ref[...], preferred_element_type=F32)
           + jnp.dot(at_ref[0, sl, :], woa_ref[...], preferred_element_type=F32) for p, sl in enumerate(slabs)]
    h1 = [x_ref[0, sl, :] + _rms(mix[p], npost_ref[...]) for p, sl in enumerate(slabs)]
    u = [_rms(h, nmpre_ref[...]).astype(BF16) for h in h1]

    def up(p, j):
        return jnp.dot(u[p], wup_ref[:, j * FF_TILE:(j + 1) * FF_TILE], preferred_element_type=F32)

    n_ff = D_FF // FF_TILE
    f = [None] * ROW_SPLIT
    a = [up(p, 0) for p in range(ROW_SPLIT)]
    for j in range(n_ff):
        a_next = [up(p, j + 1) for p in range(ROW_SPLIT)] if j + 1 < n_ff else None
        for p in range(ROW_SPLIT):
            act = jnp.maximum(a[p], 0.0)
            part = jnp.dot((act * act).astype(BF16), wdn_ref[j * FF_TILE:(j + 1) * FF_TILE, :],
                           preferred_element_type=F32)
            f[p] = part if f[p] is None else f[p] + part
        a = a_next
    for p, sl in enumerate(slabs):
        out_ref[0, sl, :] = h1[p] + _rms(f[p], nmpost_ref[...])


def _out_mlp(x, o_f, o_b, z, at, dn_out_norm, w_out, norm_mix_post, norm_mlp_pre, w_up, w_down, norm_mlp_post):
    bsz, seq, _ = x.shape
    off = PAD_ROWS // ROW_TILE
    dnw = jnp.tile(dn_out_norm.astype(F32), DN_HEADS)[None]
    ones = _block_diag_ones(DN_WIDTH, DN_HEAD_DIM)
    wo = w_out.astype(BF16)
    consts = (dnw, ones, wo[:DN_WIDTH], wo[DN_WIDTH:], norm_mix_post.astype(F32)[None],
              norm_mlp_pre.astype(F32)[None], w_up.astype(BF16), w_down.astype(BF16),
              norm_mlp_post.astype(F32)[None])

    def full(a):
        return pl.BlockSpec(a.shape, lambda b, i: (0,) * a.ndim, pipeline_mode=pl.Buffered(1))

    def real(w):
        return pl.BlockSpec((1, ROW_TILE, w), lambda b, i: (b, i, 0))

    def padded(w):
        return pl.BlockSpec((1, ROW_TILE, w), lambda b, i: (b, i + off, 0))

    return pl.pallas_call(
        _out_mlp_kernel,
        grid=(bsz, seq // ROW_TILE),
        in_specs=[real(D_MODEL), padded(DN_WIDTH), padded(DN_WIDTH), padded(DN_WIDTH), real(AT_WIDTH)]
                 + [full(a) for a in consts],
        out_specs=real(D_MODEL),
        out_shape=jax.ShapeDtypeStruct((bsz, seq, D_MODEL), x.dtype),
        compiler_params=_cparams(("parallel", "parallel")),
        name="out_mlp",
    )(x, o_f, o_b, z, at, *consts)


def kernel(x, meta_tokens, w_in, conv_w, a_log, dt_bias, dn_out_norm, q_norm, k_norm, w_out,
           norm_mix_pre, norm_mix_post, w_up, w_down, norm_mlp_pre, norm_mlp_post):
    assert w_in.shape[0] == 1, "single-layer block: meta-token outputs are not materialised"
    bsz, seq, d = x.shape
    assert d == D_MODEL and seq % ROW_TILE == 0 and seq % GRID_W == 0
    dq, dk, dv, gates, z, aq, ak, av = _in_projection(x, meta_tokens, w_in[0], norm_mix_pre[0], q_norm[0],
                                                      k_norm[0], conv_w[0], a_log[0], dt_bias[0])
    o_f, o_b = _delta_rule(dq, dk, dv, gates, seq)
    at = _attention(aq, ak, av, seq)
    return _out_mlp(x, o_f, o_b, z, at, dn_out_norm[0], w_out[0], norm_mix_post[0], norm_mlp_pre[0],
                    w_up[0], w_down[0], norm_mlp_post[0])
```

```python
import functools
import math

import numpy as np
import jax
import jax.numpy as jnp
from jax import lax
from jax.experimental import pallas as pl
from jax.experimental.pallas import tpu as pltpu

F32 = jnp.float32
BF16 = jnp.bfloat16

D_MODEL = 1024
N_META = 16
GRID_W = 64
DN_HEADS = 4
DN_HEAD_DIM = 128
DN_WIDTH = DN_HEADS * DN_HEAD_DIM
CONV_W = 5
AT_Q_HEADS = 8
AT_KV_HEADS = 2
AT_HEAD_DIM = 64
AT_WIDTH = AT_Q_HEADS * AT_HEAD_DIM
AT_KV_WIDTH = AT_KV_HEADS * AT_HEAD_DIM
ROPE_THETA = 10000.0
ROPE_AXIS_DIM = AT_HEAD_DIM // 2
D_FF = 4 * D_MODEL
EPS = 1e-6

LANES = 128
SUBLANES = 8
MXU_DIM = 256
HALO = 8
ROW_TILE = 512
ROW_SPLIT = 2
IN_SPLIT = 2
PAD_ROWS = ROW_TILE
DN_CHUNK = 128
META_CHUNK = PAD_ROWS // DN_CHUNK - 1
GATE_W = LANES
VMEM_LIMIT = 56 * 1024 * 1024


def _cparams(sem):
    return pltpu.CompilerParams(dimension_semantics=sem, vmem_limit_bytes=VMEM_LIMIT)


def _split_bf16(x):
    hi = x.astype(BF16)
    lo = (x - hi.astype(F32)).astype(BF16)
    return hi, lo


def _group_mean(sq, ones_bd):
    return jnp.dot(sq.astype(BF16), ones_bd, preferred_element_type=F32)


def _block_diag_ones(width, group):
    idx = np.arange(width) // group
    return jnp.asarray((idx[:, None] == idx[None, :]).astype(np.float32) / group, dtype=BF16)


def _sigmoid(x):
    return 1.0 / (1.0 + jnp.exp(-x))


def _rope(x, cos, sin_signed, width):
    half = ROPE_AXIS_DIM // 2
    lane = lax.broadcasted_iota(jnp.int32, x.shape, 1)
    partner = jnp.where((lane % ROPE_AXIS_DIM) < half,
                        pltpu.roll(x, width - half, 1),
                        pltpu.roll(x, half, 1))
    return x * cos + partner * sin_signed


def _softplus(x):
    return jnp.maximum(x, 0.0) + jnp.log1p(jnp.exp(-jnp.abs(x)))


def _inproj_kernel(x_ref, xprev_ref, xnext_ref, metap_ref, wpre_ref, wqkv_ref, wz_ref, wba_ref, wq_ref, wk_ref,
                   wv_ref, qn_ref, kn_ref, cw_ref, alog_ref, dtb_ref, cos_ref, sin_ref, onesq_ref, onesk_ref,
                   onesd_ref, dq_ref, dk_ref, dv_ref, g_ref, z_ref, aq_ref, ak_ref, av_ref):
    i = pl.program_id(1)
    last = pl.num_programs(1) - 1
    rows = ROW_TILE // IN_SPLIT
    ext = rows + 2 * HALO
    slabs = [slice(p * rows, (p + 1) * rows) for p in range(IN_SPLIT)]

    hc = jnp.where(i == 0, metap_ref[...], x_ref[0])
    hp = jnp.where(i == 0, 0.0, jnp.where(i == 1, metap_ref[ROW_TILE - HALO:, :], xprev_ref[0]))
    hn = jnp.where(i == last, 0.0, xnext_ref[0])
    he = jnp.concatenate([hp, hc, hn], axis=0)

    def project(p):
        hs = he[p * rows:p * rows + ext]
        ms = jnp.mean(hs * hs, axis=-1, keepdims=True)
        ue = hs * lax.rsqrt(ms + EPS) * wpre_ref[...]
        uc = ue[HALO:HALO + rows].astype(BF16)
        z_ref[0, slabs[p], :] = jnp.dot(uc, wz_ref[...], preferred_element_type=F32)
        return (jnp.dot(ue.astype(BF16), wqkv_ref[...], preferred_element_type=F32),
                jnp.dot(uc, wba_ref[...], preferred_element_type=F32),
                jnp.dot(uc, wq_ref[...], preferred_element_type=F32),
                jnp.dot(uc, wk_ref[...], preferred_element_type=F32),
                jnp.dot(uc, wv_ref[...], preferred_element_type=F32))

    reps = AT_WIDTH // LANES
    tail = jnp.where(lax.broadcasted_iota(jnp.int32, (rows, AT_HEAD_DIM), 1) == 0, 1.0, 0.0)
    nxt = project(0)
    for p, sl in enumerate(slabs):
        qkv, raw, q, k, v = nxt
        if p + 1 < IN_SPLIT:
            nxt = project(p + 1)

        acc = None
        for j in range(CONV_W):
            shift = (CONV_W // 2 - j) % ext
            shifted = qkv if shift == 0 else pltpu.roll(qkv, shift, 0)
            term = shifted[HALO:HALO + rows] * cw_ref[j:j + 1, :]
            acc = term if acc is None else acc + term
        y = acc * _sigmoid(acc)

        ones = onesd_ref[...]
        dq = y[:, :DN_WIDTH]
        dk = y[:, DN_WIDTH:2 * DN_WIDTH]
        dq = dq * lax.rsqrt(_group_mean(dq * dq, ones) * DN_HEAD_DIM + EPS) * (DN_HEAD_DIM ** -0.5)
        dk = dk * lax.rsqrt(_group_mean(dk * dk, ones) * DN_HEAD_DIM + EPS)
        dq_ref[0, sl, :] = dq
        dk_ref[0, sl, :] = dk
        dv_ref[0, sl, :] = y[:, 2 * DN_WIDTH:]

        row = lax.broadcasted_iota(jnp.int32, (rows, 1), 0) + p * rows
        valid = jnp.logical_or(i > 0, row >= PAD_ROWS - N_META)

        lane = lax.broadcasted_iota(jnp.int32, raw.shape, 1)
        g = -jnp.exp(alog_ref[...]) * _softplus(raw + dtb_ref[...])
        gates = jnp.where(lane < 2 * DN_HEADS, _sigmoid(raw), jnp.where(lane < 4 * DN_HEADS, g, 0.0))
        g_ref[0, sl, :] = jnp.where(valid, gates, 0.0)

        cos2 = cos_ref[sl, :]
        sin2 = sin_ref[sl, :]
        qs = q * lax.rsqrt(_group_mean(q * q, onesq_ref[...]) + EPS) * qn_ref[...]
        qs = _rope(qs, jnp.concatenate([cos2] * reps, axis=1), jnp.concatenate([sin2] * reps, axis=1), AT_WIDTH)
        aq_ref[0, :, sl] = (qs * (AT_HEAD_DIM ** -0.5 * math.log2(math.e))).T.astype(BF16)

        ks = k * lax.rsqrt(_group_mean(k * k, onesk_ref[...]) + EPS) * kn_ref[...]
        ks = _rope(ks, cos2, sin2, AT_KV_WIDTH).astype(BF16)
        ak_ref[0, sl, :] = jnp.concatenate([ks[:, :AT_HEAD_DIM], ks[:, :AT_HEAD_DIM],
                                            ks[:, AT_HEAD_DIM:], ks[:, AT_HEAD_DIM:]], axis=1)

        av_ref[0, :, sl] = jnp.concatenate([v[:, :AT_HEAD_DIM], tail, v[:, AT_HEAD_DIM:], tail],
                                           axis=1).T.astype(BF16)


def _rope_tables(seq):
    f = ROPE_AXIS_DIM // 2
    n_rows = seq // GRID_W
    freqs = ROPE_THETA ** (-jnp.arange(f, dtype=F32) / f)
    ang_r = jnp.arange(n_rows, dtype=F32)[:, None] * freqs
    ang_c = jnp.arange(GRID_W, dtype=F32)[:, None] * freqs

    def grid(tab_r, tab_c, sign):
        tr = jnp.broadcast_to(tab_r[:, None, :], (n_rows, GRID_W, f))
        tc = jnp.broadcast_to(tab_c[None, :, :], (n_rows, GRID_W, f))
        head = jnp.concatenate([sign * tr, tr, sign * tc, tc], axis=-1).reshape(seq, 4 * f)
        return jnp.concatenate([head, head], axis=1)

    cos = grid(jnp.cos(ang_r), jnp.cos(ang_c), 1.0)
    sin = grid(jnp.sin(ang_r), jnp.sin(ang_c), -1.0)
    cos = jnp.concatenate([jnp.ones((PAD_ROWS, cos.shape[1]), F32), cos], axis=0)
    sin = jnp.concatenate([jnp.zeros((PAD_ROWS, sin.shape[1]), F32), sin], axis=0)
    return cos, sin


def _in_projection(x, meta_tokens, w_in, norm_pre, q_norm, k_norm, conv_w, a_log, dt_bias):
    bsz, seq, _ = x.shape
    lp = PAD_ROWS + seq
    nt = lp // ROW_TILE
    per = ROW_TILE // HALO
    cw = jnp.pad(conv_w.astype(F32), ((0, 8 - CONV_W), (0, 0)))
    gate_pad = (2 * DN_HEADS, GATE_W - 4 * DN_HEADS)
    alog = jnp.pad(a_log.astype(F32).reshape(-1), gate_pad)[None]
    dtb = jnp.pad(dt_bias.astype(F32).reshape(-1), gate_pad)[None]
    ones_d = _block_diag_ones(DN_WIDTH, DN_HEAD_DIM)
    o = np.cumsum([0, DN_WIDTH * 3, DN_WIDTH, 2 * DN_HEADS, 2 * DN_HEADS, AT_WIDTH, AT_KV_WIDTH, AT_KV_WIDTH])
    wb = w_in.astype(BF16)
    w_qkv, w_z = wb[:, o[0]:o[1]], wb[:, o[1]:o[2]]
    w_ba = jnp.pad(wb[:, o[2]:o[4]], ((0, 0), (0, GATE_W - 4 * DN_HEADS)))
    w_q, w_k, w_v = wb[:, o[4]:o[5]], wb[:, o[5]:o[6]], wb[:, o[6]:o[7]]
    metap = jnp.pad(meta_tokens.astype(F32), ((PAD_ROWS - N_META, 0), (0, 0)))
    cos2, sin2 = _rope_tables(seq)
    qn = jnp.tile(q_norm.astype(F32), AT_Q_HEADS)[None]
    kn = jnp.tile(k_norm.astype(F32), AT_KV_HEADS)[None]

    def full(a):
        return pl.BlockSpec(a.shape, lambda b, i: (0,) * a.ndim)

    def rows(width):
        return pl.BlockSpec((1, ROW_TILE, width), lambda b, i: (b, i, 0))

    wpre = norm_pre.astype(F32)[None]
    ones_q = _block_diag_ones(AT_WIDTH, AT_HEAD_DIM)
    ones_k = _block_diag_ones(AT_KV_WIDTH, AT_HEAD_DIM)
    consts = (metap, wpre, w_qkv, w_z, w_ba, w_q, w_k, w_v, qn, kn, cw, alog, dtb)

    def cols(height):
        return pl.BlockSpec((1, height, ROW_TILE), lambda b, i: (b, 0, i))

    x_specs = [pl.BlockSpec((1, ROW_TILE, D_MODEL), lambda b, i: (b, jnp.maximum(i - 1, 0), 0)),
               pl.BlockSpec((1, HALO, D_MODEL), lambda b, i: (b, jnp.maximum((i - 1) * per - 1, 0), 0)),
               pl.BlockSpec((1, HALO, D_MODEL), lambda b, i: (b, jnp.minimum(i * per, seq // HALO - 1), 0))]
    q_spec = pl.BlockSpec((1, AT_WIDTH, ROW_TILE), lambda b, i: (b, 0, jnp.maximum(i - 1, 0)))
    out_specs = [rows(DN_WIDTH)] * 3 + [rows(GATE_W), rows(DN_WIDTH), q_spec, rows(2 * LANES), cols(2 * LANES)]
    out_shape = [jax.ShapeDtypeStruct((bsz, lp, DN_WIDTH), F32)] * 3 + [
        jax.ShapeDtypeStruct((bsz, lp, GATE_W), F32),
        jax.ShapeDtypeStruct((bsz, lp, DN_WIDTH), F32),
        jax.ShapeDtypeStruct((bsz, AT_WIDTH, seq), BF16),
        jax.ShapeDtypeStruct((bsz, lp, 2 * LANES), BF16),
        jax.ShapeDtypeStruct((bsz, 2 * LANES, lp), BF16)]
    return pl.pallas_call(
        _inproj_kernel,
        grid=(bsz, nt),
        in_specs=x_specs + [full(a) for a in consts]
                 + [pl.BlockSpec((ROW_TILE, LANES), lambda b, i: (i, 0))] * 2
                 + [full(ones_q), full(ones_k), full(ones_d)],
        out_specs=out_specs,
        out_shape=out_shape,
        compiler_params=_cparams(("parallel", "arbitrary")),
        name="in_projection",
    )(x, x, x, *consts, cos2, sin2, ones_q, ones_k, ones_d)


def _dot(a, b):
    return jnp.dot(a.astype(BF16), b.astype(BF16), preferred_element_type=F32)


def _dot_nt(a, b):
    return lax.dot_general(a.astype(BF16), b.astype(BF16), (((1,), (1,)), ((), ())),
                           preferred_element_type=F32)


INV_BLOCK = 8


def _unit_tri_inverses(neg_as, row, col):
    same = lambda w: (row // w) == (col // w)
    eye = jnp.where(row == col, 1.0, 0.0)
    xs = [jnp.where(same(INV_BLOCK), a, 0.0) for a in neg_as]
    ps = [eye + x for x in xs]
    for _ in range(int(math.log2(INV_BLOCK)) - 1):
        xs = [_dot(x, x) for x in xs]
        ps = [p + _dot(p, x) for p, x in zip(ps, xs)]
    w = INV_BLOCK
    while w < DN_CHUNK:
        band = jnp.logical_and(same(2 * w), jnp.logical_not(same(w)))
        ts = [_dot(p, jnp.where(band, a, 0.0)) for p, a in zip(ps, neg_as)]
        ps = [p + _dot(t, p) for p, t in zip(ps, ts)]
        w *= 2
    return [p - eye for p in ps]


def _delta_kernel(qf_ref, kf_ref, vf_ref, gf_ref, qb_ref, kb_ref, vb_ref, gb_ref,
                  of_ref, ob_ref, s_ref):
    s = pl.program_id(1)

    @pl.when(s == 0)
    def _():
        s_ref[...] = jnp.zeros_like(s_ref)

    c = DN_CHUNK
    hd = DN_HEAD_DIM
    nb = qf_ref.shape[0]
    row = lax.broadcasted_iota(jnp.int32, (c, c), 0)
    col = lax.broadcasted_iota(jnp.int32, (c, c), 1)
    dirs = ((qf_ref, kf_ref, vf_ref, gf_ref, of_ref), (qb_ref, kb_ref, vb_ref, gb_ref, ob_ref))
    scans = [(bi, d) for bi in range(nb) for d in range(2)]
    streams = [(e, h) for e in range(len(scans)) for h in range(DN_HEADS)]

    incl, strict, lmat = [], [], []
    for d in range(2):
        order = col - row if d == 0 else (col - row) * jnp.where(s == 0, 1, -1)
        incl.append(order <= 0)
        strict.append(order < 0)
        lmat.append(jnp.where(incl[d], 1.0, 0.0).astype(BF16))

    gates, gc_all, dm_all = [], [], []
    for bi, d in scans:
        g = dirs[d][3][bi]
        gates.append(g)
        g_hi, g_lo = _split_bf16(g)
        gc_all.append(jnp.dot(lmat[d], g_hi, preferred_element_type=F32)
                      + jnp.dot(lmat[d], g_lo, preferred_element_type=F32))
        parts = []
        for h in range(DN_HEADS):
            jg = 2 * DN_HEADS + d * DN_HEADS + h
            parts.extend(_split_bf16(jnp.where(strict[d], g[:, jg:jg + 1], 0.0)))
        dm = jnp.dot(lmat[d], jnp.concatenate(parts, axis=1), preferred_element_type=F32)
        dm_all.append([dm[:, (2 * h) * c:(2 * h + 1) * c] + dm[:, (2 * h + 1) * c:(2 * h + 2) * c]
                       for h in range(DN_HEADS)])

    beta, gc, gc_last, decay, egc, q, k, v = [], [], [], [], [], [], [], []
    for e, h in streams:
        bi, d = scans[e]
        jb = d * DN_HEADS + h
        jg = 2 * DN_HEADS + jb
        sl = slice(h * hd, (h + 1) * hd)
        beta.append(gates[e][:, jb:jb + 1])
        gcs = gc_all[e][:, jg:jg + 1]
        gc.append(gcs)
        gc_last.append(gcs[c - 1:c, :] if d == 0 else jnp.where(s == 0, gcs[c - 1:c, :], gcs[0:1, :]))
        decay.append(jnp.where(incl[d], jnp.exp(dm_all[e][h]), 0.0))
        egc.append(jnp.exp(gcs))
        q.append(dirs[d][0][bi, :, sl])
        k.append(dirs[d][1][bi, :, sl])
        v.append(dirs[d][2][bi, :, sl])

    n = len(streams)
    tri = [strict[scans[e][1]] for e, _ in streams]
    qkk = [_dot_nt(jnp.concatenate([q[i], k[i]], axis=0), k[i]) for i in range(n)]
    qk = [qkk[i][:c] * decay[i] for i in range(n)]
    neg_a = [jnp.where(tri[i], -(qkk[i][c:] * decay[i] * beta[i]), 0.0) for i in range(n)]
    tn = _unit_tri_inverses(neg_a, row, col)
    rhs = [jnp.concatenate([v[i] * beta[i], k[i] * (beta[i] * egc[i])], axis=1) for i in range(n)]
    uw = [rhs[i] + _dot(tn[i], rhs[i]) for i in range(n)]
    state = [s_ref[i] for i in range(n)]
    wq = [_dot(jnp.concatenate([uw[i][:, hd:], q[i] * egc[i]], axis=0), state[i]) for i in range(n)]
    v_new = [uw[i][:, :hd] - wq[i][:c] for i in range(n)]
    o = [wq[i][c:] + _dot(qk[i], v_new[i]) for i in range(n)]
    for i, (e, h) in enumerate(streams):
        bi, d = scans[e]
        dirs[d][4][bi, :, h * hd:(h + 1) * hd] = o[i]
    k_dec = [k[i] * jnp.exp(gc_last[i] - gc[i]) for i in range(n)]
    s_new = [state[i] * jnp.exp(gc_last[i]) + _dot(k_dec[i].T, v_new[i]) for i in range(n)]
    for i in range(n):
        s_ref[i] = s_new[i]


DN_BATCH = 2


def _delta_rule(q, k, v, gates, seq):
    bsz, lp, _ = q.shape
    ns = 1 + seq // DN_CHUNK
    nb = math.gcd(DN_BATCH, bsz)

    def fwd(b, s):
        return (b, META_CHUNK + s, 0)

    def bwd(b, s):
        return (b, jnp.where(s == 0, META_CHUNK, META_CHUNK + ns - s), 0)

    def spec(w, imap):
        return pl.BlockSpec((nb, DN_CHUNK, w), imap)

    ins = [spec(DN_WIDTH, fwd)] * 3 + [spec(GATE_W, fwd)] + [spec(DN_WIDTH, bwd)] * 3 + [spec(GATE_W, bwd)]
    return pl.pallas_call(
        _delta_kernel,
        grid=(bsz // nb, ns),
        in_specs=ins,
        out_specs=[spec(DN_WIDTH, fwd), spec(DN_WIDTH, bwd)],
        out_shape=[jax.ShapeDtypeStruct((bsz, lp, DN_WIDTH), F32)] * 2,
        scratch_shapes=[pltpu.VMEM((nb * 2 * DN_HEADS, DN_HEAD_DIM, DN_HEAD_DIM), F32)],
        compiler_params=_cparams(("parallel", "arbitrary")),
        name="delta_rule",
    )(q, k, v, gates, q, k, v, gates)


KEY_START = PAD_ROWS - MXU_DIM
GROUP = AT_Q_HEADS // AT_KV_HEADS
Q_TILE = 512
KV_GROUPS_PER_STEP = 2
UNITS_PER_TRIP = 40


def _kv_tile(n_keys):
    units = n_keys // MXU_DIM
    best = max(u for u in range(1, 4) if units % u == 0)
    return best * MXU_DIM


def _attn_kernel(qt_ref, k_ref, vt_ref, o_ref, qp_ref, m_ref, acc_ref, s_ref, *, tk, nkb):
    tq = qt_ref.shape[2]
    hd = AT_HEAD_DIM
    nh = qt_ref.shape[1] // hd
    for j in range(nh):
        qp_ref[j, :hd, :] = qt_ref[0, j * hd:(j + 1) * hd, :]
        qp_ref[j, hd:, :] = jnp.zeros((LANES - hd, tq), BF16)

    def start_of(kb):
        start = KEY_START + kb * tk
        return start if isinstance(kb, int) else pl.multiple_of(start, LANES)

    def group(j):
        g = j // GROUP
        return slice(g * LANES, (g + 1) * LANES)

    def scores(j, kb):
        return jnp.dot(k_ref[0, pl.ds(start_of(kb), tk), group(j)], qp_ref[j], preferred_element_type=F32)

    def col_max(s):
        m = s[:SUBLANES]
        for t in range(1, tk // SUBLANES):
            m = jnp.maximum(m, s[t * SUBLANES:(t + 1) * SUBLANES])
        return jnp.broadcast_to(jnp.max(m, axis=0, keepdims=True), (SUBLANES, tq))

    def rows(m, n):
        return jnp.concatenate([m] * (n // SUBLANES), axis=0)

    def unit(kb, j, first):
        if j + 1 < nh:
            nxt_kb = kb
        elif isinstance(kb, int):
            nxt_kb = min(kb + 1, nkb - 1)
        else:
            nxt_kb = jnp.minimum(kb + 1, nkb - 1)
        s_ref[(j + 1) % 2] = scores((j + 1) % nh, nxt_kb)
        s = s_ref[j % 2]
        vt = vt_ref[0, group(j), pl.ds(start_of(kb), tk)]
        if first:
            is_key = lax.broadcasted_iota(jnp.int32, (tk, tq), 0) >= PAD_ROWS - N_META - KEY_START
            s = jnp.where(is_key, s, -jnp.inf)
            m_new = col_max(s)
            p = jnp.exp2(s - rows(m_new, tk))
            acc_ref[j] = jnp.dot(vt, p.astype(BF16), preferred_element_type=F32)
        else:
            m_prev = m_ref[j]
            m_new = jnp.maximum(m_prev, col_max(s))
            alpha = jnp.exp2(m_prev - m_new)
            p = jnp.exp2(s - rows(m_new, tk))
            acc_ref[j] = rows(alpha, LANES) * acc_ref[j] + jnp.dot(vt, p.astype(BF16), preferred_element_type=F32)
        m_ref[j] = m_new

    s_ref[0] = scores(0, 0)
    for j in range(nh):
        unit(0, j, True)

    def body(kb, carry):
        for j in range(nh):
            unit(kb, j, False)
        return carry

    trips = nkb - 1
    lax.fori_loop(1, nkb, body, 0, unroll=max(u for u in range(1, UNITS_PER_TRIP // nh + 1) if trips % u == 0))

    outs = []
    for j in range(nh):
        acc = acc_ref[j]
        outs.append(acc[:hd] * (1.0 / acc[hd:hd + 1]))
    o_ref[0] = jnp.concatenate(outs, axis=0).T.astype(o_ref.dtype)


def _attention(aqt, ak, avt, seq):
    bsz, lp, _ = ak.shape
    tq = math.gcd(Q_TILE, seq)
    n_keys = lp - KEY_START
    tk = _kv_tile(n_keys)
    ng = AT_KV_HEADS // KV_GROUPS_PER_STEP
    nh = KV_GROUPS_PER_STEP * GROUP
    assert nh % 2 == 0
    return pl.pallas_call(
        functools.partial(_attn_kernel, tk=tk, nkb=n_keys // tk),
        grid=(bsz, ng, seq // tq),
        in_specs=[pl.BlockSpec((1, nh * AT_HEAD_DIM, tq), lambda b, g, i: (b, g, i)),
                  pl.BlockSpec((1, lp, KV_GROUPS_PER_STEP * LANES), lambda b, g, i: (b, 0, g)),
                  pl.BlockSpec((1, KV_GROUPS_PER_STEP * LANES, lp), lambda b, g, i: (b, g, 0))],
        out_specs=pl.BlockSpec((1, tq, nh * AT_HEAD_DIM), lambda b, g, i: (b, i, g)),
        out_shape=jax.ShapeDtypeStruct((bsz, seq, AT_WIDTH), BF16),
        scratch_shapes=[pltpu.VMEM((nh, LANES, tq), BF16),
                        pltpu.VMEM((nh, SUBLANES, tq), F32),
                        pltpu.VMEM((nh, LANES, tq), F32),
                        pltpu.VMEM((2, tk, tq), F32)],
        compiler_params=_cparams(("parallel", "parallel", "parallel")),
        name="gqa_attention",
    )(aqt, ak, avt)


FF_TILE = 1024


def _rms(x, w):
    return x * lax.rsqrt(jnp.mean(x * x, axis=-1, keepdims=True) + EPS) * w


def _out_mlp_kernel(x_ref, of_ref, ob_ref, z_ref, at_ref, dnw_ref, ones_ref, wod_ref, woa_ref, npost_ref,
                    nmpre_ref, wup_ref, wdn_ref, nmpost_ref, out_ref):
    rows = ROW_TILE // ROW_SPLIT
    slabs = [slice(p * rows, (p + 1) * rows) for p in range(ROW_SPLIT)]

    dn = []
    for sl in slabs:
        o = of_ref[0, sl, :] + ob_ref[0, sl, :]
        o = o * lax.rsqrt(_group_mean(o * o, ones_ref[...]) + EPS) * dnw_ref[...]
        z = z_ref[0, sl, :]
        dn.append((o * (z * _sigmoid(z))).astype(BF16))
    mix = [jnp.dot(dn[p], wod_ref[...], preferred_element_type=F32)
           + jnp.dot(at_ref[0, sl, :], woa_ref[...], preferred_element_type=F32) for p, sl in enumerate(slabs)]
    h1 = [x_ref[0, sl, :] + _rms(mix[p], npost_ref[...]) for p, sl in enumerate(slabs)]
    u = [_rms(h, nmpre_ref[...]).astype(BF16) for h in h1]

    def up(p, j):
        return jnp.dot(u[p], wup_ref[:, j * FF_TILE:(j + 1) * FF_TILE], preferred_element_type=F32)

    n_ff = D_FF // FF_TILE
    f = [None] * ROW_SPLIT
    a = [up(p, 0) for p in range(ROW_SPLIT)]
    for j in range(n_ff):
        a_next = [up(p, j + 1) for p in range(ROW_SPLIT)] if j + 1 < n_ff else None
        for p in range(ROW_SPLIT):
            act = jnp.maximum(a[p], 0.0)
            part = jnp.dot((act * act).astype(BF16), wdn_ref[j * FF_TILE:(j + 1) * FF_TILE, :],
                           preferred_element_type=F32)
            f[p] = part if f[p] is None else f[p] + part
        a = a_next
    for p, sl in enumerate(slabs):
        out_ref[0, sl, :] = h1[p] + _rms(f[p], nmpost_ref[...])


def _out_mlp(x, o_f, o_b, z, at, dn_out_norm, w_out, norm_mix_post, norm_mlp_pre, w_up, w_down, norm_mlp_post):
    bsz, seq, _ = x.shape
    off = PAD_ROWS // ROW_TILE
    dnw = jnp.tile(dn_out_norm.astype(F32), DN_HEADS)[None]
    ones = _block_diag_ones(DN_WIDTH, DN_HEAD_DIM)
    wo = w_out.astype(BF16)
    consts = (dnw, ones, wo[:DN_WIDTH], wo[DN_WIDTH:], norm_mix_post.astype(F32)[None],
              norm_mlp_pre.astype(F32)[None], w_up.astype(BF16), w_down.astype(BF16),
              norm_mlp_post.astype(F32)[None])

    def full(a):
        return pl.BlockSpec(a.shape, lambda b, i: (0,) * a.ndim, pipeline_mode=pl.Buffered(1))

    def real(w):
        return pl.BlockSpec((1, ROW_TILE, w), lambda b, i: (b, i, 0))

    def padded(w):
        return pl.BlockSpec((1, ROW_TILE, w), lambda b, i: (b, i + off, 0))

    return pl.pallas_call(
        _out_mlp_kernel,
        grid=(bsz, seq // ROW_TILE),
        in_specs=[real(D_MODEL), padded(DN_WIDTH), padded(DN_WIDTH), padded(DN_WIDTH), real(AT_WIDTH)]
                 + [full(a) for a in consts],
        out_specs=real(D_MODEL),
        out_shape=jax.ShapeDtypeStruct((bsz, seq, D_MODEL), x.dtype),
        compiler_params=_cparams(("parallel", "parallel")),
        name="out_mlp",
    )(x, o_f, o_b, z, at, *consts)


def kernel(x, meta_tokens, w_in, conv_w, a_log, dt_bias, dn_out_norm, q_norm, k_norm, w_out,
           norm_mix_pre, norm_mix_post, w_up, w_down, norm_mlp_pre, norm_mlp_post):
    assert w_in.shape[0] == 1, "single-layer block: meta-token outputs are not materialised"
    bsz, seq, d = x.shape
    assert d == D_MODEL and seq % ROW_TILE == 0 and seq % GRID_W == 0
    dq, dk, dv, gates, z, aq, ak, av = _in_projection(x, meta_tokens, w_in[0], norm_mix_pre[0], q_norm[0],
                                                      k_norm[0], conv_w[0], a_log[0], dt_bias[0])
    o_f, o_b = _delta_rule(dq, dk, dv, gates, seq)
    at = _attention(aq, ak, av, seq)
    return _out_mlp(x, o_f, o_b, z, at, dn_out_norm[0], w_out[0], norm_mix_post[0], norm_mlp_pre[0],
                    w_up[0], w_down[0], norm_mlp_post[0])
```
